```python
import math
import jax, jax.numpy as jnp
from jax import lax
import numpy as np

D_MODEL = 1024
BATCH = 8
SEQ = 8192
DEPTH = 4

POOL_WINDOWS = (2, 4, 8, 16)
N_POOL_GROUPS = len(POOL_WINDOWS)
POOL_GROUP_DIM = D_MODEL // 8
POOL_W = N_POOL_GROUPS * POOL_GROUP_DIM
FOX_HEADS = 8
FOX_HEAD_DIM = 64
FOX_W = FOX_HEADS * FOX_HEAD_DIM
Q_BLOCK = 128
OFF_POOL = 0
OFF_Q = OFF_POOL + POOL_W
OFF_K = OFF_Q + FOX_W
OFF_V = OFF_K + FOX_W
OFF_F = OFF_V + FOX_W
OFF_GP = OFF_F + FOX_HEADS
OFF_GF = OFF_GP + D_MODEL
IN_W = OFF_GF + D_MODEL
N_MEM = 256
X_HEADS = 4
X_HEAD_DIM = 128
X_W = X_HEADS * X_HEAD_DIM
D_FF = 2816
CONV_WIDTH = 3
RMS_EPS = 1e-6

kernel_name = "hybrid_pool_fox_memxattn_convffn"


def rms_norm(x, g):
    xf = x.astype(jnp.float32)
    y = xf * lax.rsqrt(jnp.mean(xf * xf, axis=-1, keepdims=True) + RMS_EPS)
    return (y * g.astype(jnp.float32)).astype(x.dtype)


def shift_right(z, n):
    return jnp.pad(z, ((0, 0), (n, 0), (0, 0)))[:, : z.shape[1]]


def pool_mixer(u, pool_w, pool_scale):
    B, S, _ = u.shape
    uf = u.astype(jnp.float32)
    cs = jnp.cumsum(uf, axis=1)
    t = jnp.arange(S)
    outs = []
    for g, w in enumerate(POOL_WINDOWS):
        sl = slice(g * POOL_GROUP_DIM, (g + 1) * POOL_GROUP_DIM)
        csg = cs[..., sl]
        cnt = jnp.minimum(t + 1, w).astype(jnp.float32)[None, :, None]
        outs.append((csg - shift_right(csg, w)) / cnt - uf[..., sl])
    pooled = jnp.stack(outs, axis=2).astype(u.dtype)
    mixed = jnp.einsum("bsgc,gcd->bsgd", pooled, pool_w).reshape(B, S, POOL_W)
    return mixed * pool_scale


def fox_attention(q, k, v, log_f):
    B, S, H, Dh = q.shape
    nb = S // Q_BLOCK
    scale = 1.0 / math.sqrt(Dh)
    cT = jnp.cumsum(log_f.astype(jnp.float32), axis=1).transpose(0, 2, 1)
    qb = q.reshape(B, nb, Q_BLOCK, H, Dh).transpose(1, 0, 2, 3, 4)
    cb = cT.reshape(B, H, nb, Q_BLOCK).transpose(2, 0, 1, 3)
    kpos = jnp.arange(S)

    def one_block(args):
        qi, ci, bi = args
        s = jnp.einsum("bqhd,bkhd->bhqk", qi, k, preferred_element_type=jnp.float32) * scale
        s = s + ci[..., :, None] - cT[:, :, None, :]
        qpos = bi * Q_BLOCK + jnp.arange(Q_BLOCK)
        s = jnp.where(kpos[None, :] <= qpos[:, None], s, -jnp.inf)
        p = jax.nn.softmax(s, axis=-1)
        return jnp.einsum("bhqk,bkhd->bqhd", p.astype(v.dtype), v)

    out = lax.map(one_block, (qb, cb, jnp.arange(nb)))
    return out.transpose(1, 0, 2, 3, 4).reshape(B, S, H * Dh)


def mem_attention(h, mem_n, w_xq, w_xkv, w_xo):
    B, S, _ = h.shape
    M = mem_n.shape[1]
    q = (h @ w_xq).reshape(B, S, X_HEADS, X_HEAD_DIM)
    kv = mem_n @ w_xkv
    k = kv[..., :X_W].reshape(B, M, X_HEADS, X_HEAD_DIM)
    v = kv[..., X_W:].reshape(B, M, X_HEADS, X_HEAD_DIM)
    s = jnp.einsum("bqhd,bmhd->bhqm", q, k, preferred_element_type=jnp.float32) / math.sqrt(X_HEAD_DIM)
    p = jax.nn.softmax(s, axis=-1)
    o = jnp.einsum("bhqm,bmhd->bqhd", p.astype(v.dtype), v).reshape(B, S, X_W)
    return o @ w_xo


def conv_ffn(h, w_up, conv_w, conv_b, w_down):
    z = h @ w_up
    zc = conv_w[2] * z + conv_w[1] * shift_right(z, 1) + conv_w[0] * shift_right(z, 2) + conv_b
    g, u = zc[..., :D_FF], zc[..., D_FF:]
    return (jax.nn.gelu(g, approximate=True) * u) @ w_down


def _fwd_setup_inputs(seed: int = 0) -> dict:
    key = jax.random.key(seed)
    ks = jax.random.split(key, 32)
    L, D = DEPTH, D_MODEL
    f32 = jnp.float32

    def nrm(k, shape, fan_in):
        return jax.random.normal(k, shape, f32) * (fan_in ** -0.5)

    def gain(k):
        return 1.0 + 0.05 * jax.random.normal(k, (L, D), f32)

    b_forget = (jnp.linspace(1.0, 5.0, FOX_HEADS, dtype=f32)[None, :]
                + 0.1 * jax.random.normal(ks[5], (L, FOX_HEADS), f32))
    return {
        "x": jax.random.normal(ks[0], (BATCH, SEQ, D), f32),
        "mem": jax.random.normal(ks[1], (BATCH, N_MEM, D), f32),
        "mix_pre_g": gain(ks[2]),
        "mix_post_g": gain(ks[3]),
        "w_in": nrm(ks[4], (L, D, IN_W), D),
        "b_forget": b_forget,
        "pool_w": nrm(ks[6], (L, N_POOL_GROUPS, POOL_GROUP_DIM, POOL_GROUP_DIM), POOL_GROUP_DIM),
        "pool_scale": 1.0 + 0.1 * jax.random.normal(ks[7], (L, POOL_W), f32),
        "w_pool_br": nrm(ks[8], (L, POOL_W, D), POOL_W),
        "w_fox_br": nrm(ks[9], (L, FOX_W, D), FOX_W),
        "w_mix_out": nrm(ks[10], (L, D, D), D),
        "xa_pre_g": gain(ks[11]),
        "xa_post_g": gain(ks[12]),
        "mem_g": gain(ks[13]),
        "w_xq": nrm(ks[14], (L, D, X_W), D),
        "w_xkv": nrm(ks[15], (L, D, 2 * X_W), D),
        "w_xo": nrm(ks[16], (L, X_W, D), X_W),
        "ffn_pre_g": gain(ks[17]),
        "ffn_post_g": gain(ks[18]),
        "w_up": nrm(ks[19], (L, D, 2 * D_FF), D),
        "conv_w": nrm(ks[20], (L, CONV_WIDTH, 2 * D_FF), CONV_WIDTH),
        "conv_b": 0.01 * jax.random.normal(ks[21], (L, 2 * D_FF), f32),
        "w_down": nrm(ks[22], (L, D_FF, D), D_FF),
    }


def _fwd_reference(x, mem, mix_pre_g, mix_post_g, w_in, b_forget, pool_w, pool_scale,
              w_pool_br, w_fox_br, w_mix_out, xa_pre_g, xa_post_g, mem_g, w_xq, w_xkv,
              w_xo, ffn_pre_g, ffn_post_g, w_up, conv_w, conv_b, w_down):
    B, S, _ = x.shape
    for l in range(DEPTH):
        h = rms_norm(x, mix_pre_g[l])
        z = h @ w_in[l]
        u_pool = z[..., OFF_POOL:OFF_Q]
        q = z[..., OFF_Q:OFF_K].reshape(B, S, FOX_HEADS, FOX_HEAD_DIM)
        k = z[..., OFF_K:OFF_V].reshape(B, S, FOX_HEADS, FOX_HEAD_DIM)
        v = z[..., OFF_V:OFF_F].reshape(B, S, FOX_HEADS, FOX_HEAD_DIM)
        log_f = jax.nn.log_sigmoid((z[..., OFF_F:OFF_GP] + b_forget[l]).astype(jnp.float32))
        gate_pool = jax.nn.sigmoid(z[..., OFF_GP:OFF_GF])
        gate_fox = jax.nn.sigmoid(z[..., OFF_GF:])
        y_pool = pool_mixer(u_pool, pool_w[l], pool_scale[l]) @ w_pool_br[l]
        y_fox = fox_attention(q, k, v, log_f) @ w_fox_br[l]
        merged = gate_pool * y_pool + gate_fox * y_fox
        x = x + rms_norm(merged @ w_mix_out[l], mix_post_g[l])
        h = rms_norm(x, xa_pre_g[l])
        mem_n = rms_norm(mem, mem_g[l])
        x = x + rms_norm(mem_attention(h, mem_n, w_xq[l], w_xkv[l], w_xo[l]), xa_post_g[l])
        h = rms_norm(x, ffn_pre_g[l])
        x = x + rms_norm(conv_ffn(h, w_up[l], conv_w[l], conv_b[l], w_down[l]), ffn_post_g[l])
    return x


import jax as _jax
import jax.numpy as _jnp

TWIN_FORMAT = 'train_step'
FWD_PARAMS = ['x', 'mem', 'mix_pre_g', 'mix_post_g', 'w_in', 'b_forget', 'pool_w', 'pool_scale', 'w_pool_br', 'w_fox_br', 'w_mix_out', 'xa_pre_g', 'xa_post_g', 'mem_g', 'w_xq', 'w_xkv', 'w_xo', 'ffn_pre_g', 'ffn_post_g', 'w_up', 'conv_w', 'conv_b', 'w_down']
TWIN_WEIGHTS = ['mix_pre_g', 'mix_post_g', 'w_in', 'b_forget', 'pool_w', 'pool_scale', 'w_pool_br', 'w_fox_br', 'w_mix_out', 'xa_pre_g', 'xa_post_g', 'mem_g', 'w_xq', 'w_xkv', 'w_xo', 'ffn_pre_g', 'ffn_post_g', 'w_up', 'conv_w', 'conv_b', 'w_down']
TWIN_DIFF_INPUT = 'x'
TWIN_INPUTS = ['x', 'mem', 'mix_pre_g', 'mix_post_g', 'w_in', 'b_forget', 'pool_w', 'pool_scale', 'w_pool_br', 'w_fox_br', 'w_mix_out', 'xa_pre_g', 'xa_post_g', 'mem_g', 'w_xq', 'w_xkv', 'w_xo', 'ffn_pre_g', 'ffn_post_g', 'w_up', 'conv_w', 'conv_b', 'w_down', 'loss_target', 'm_mix_pre_g', 'm_mix_post_g', 'm_w_in', 'm_b_forget', 'm_pool_w', 'm_pool_scale', 'm_w_pool_br', 'm_w_fox_br', 'm_w_mix_out', 'm_xa_pre_g', 'm_xa_post_g', 'm_mem_g', 'm_w_xq', 'm_w_xkv', 'm_w_xo', 'm_ffn_pre_g', 'm_ffn_post_g', 'm_w_up', 'm_conv_w', 'm_conv_b', 'm_w_down', 'v_mix_pre_g', 'v_mix_post_g', 'v_w_in', 'v_b_forget', 'v_pool_w', 'v_pool_scale', 'v_w_pool_br', 'v_w_fox_br', 'v_w_mix_out', 'v_xa_pre_g', 'v_xa_post_g', 'v_mem_g', 'v_w_xq', 'v_w_xkv', 'v_w_xo', 'v_ffn_pre_g', 'v_ffn_post_g', 'v_w_up', 'v_conv_w', 'v_conv_b', 'v_w_down']
TWIN_OUTPUTS = ['loss', 'grad_x', 'grad_mix_pre_g', 'grad_mix_post_g', 'grad_w_in', 'grad_b_forget', 'grad_pool_w', 'grad_pool_scale', 'grad_w_pool_br', 'grad_w_fox_br', 'grad_w_mix_out', 'grad_xa_pre_g', 'grad_xa_post_g', 'grad_mem_g', 'grad_w_xq', 'grad_w_xkv', 'grad_w_xo', 'grad_ffn_pre_g', 'grad_ffn_post_g', 'grad_w_up', 'grad_conv_w', 'grad_conv_b', 'grad_w_down', 'delta_mix_pre_g', 'delta_mix_post_g', 'delta_w_in', 'delta_b_forget', 'delta_pool_w', 'delta_pool_scale', 'delta_w_pool_br', 'delta_w_fox_br', 'delta_w_mix_out', 'delta_xa_pre_g', 'delta_xa_post_g', 'delta_mem_g', 'delta_w_xq', 'delta_w_xkv', 'delta_w_xo', 'delta_ffn_pre_g', 'delta_ffn_post_g', 'delta_w_up', 'delta_conv_w', 'delta_conv_b', 'delta_w_down', 'new_m_mix_pre_g', 'new_m_mix_post_g', 'new_m_w_in', 'new_m_b_forget', 'new_m_pool_w', 'new_m_pool_scale', 'new_m_w_pool_br', 'new_m_w_fox_br', 'new_m_w_mix_out', 'new_m_xa_pre_g', 'new_m_xa_post_g', 'new_m_mem_g', 'new_m_w_xq', 'new_m_w_xkv', 'new_m_w_xo', 'new_m_ffn_pre_g', 'new_m_ffn_post_g', 'new_m_w_up', 'new_m_conv_w', 'new_m_conv_b', 'new_m_w_down', 'new_v_mix_pre_g', 'new_v_mix_post_g', 'new_v_w_in', 'new_v_b_forget', 'new_v_pool_w', 'new_v_pool_scale', 'new_v_w_pool_br', 'new_v_w_fox_br', 'new_v_w_mix_out', 'new_v_xa_pre_g', 'new_v_xa_post_g', 'new_v_mem_g', 'new_v_w_xq', 'new_v_w_xkv', 'new_v_w_xo', 'new_v_ffn_pre_g', 'new_v_ffn_post_g', 'new_v_w_up', 'new_v_conv_w', 'new_v_conv_b', 'new_v_w_down']
TWIN_LEAF_KINDS = {'loss': 'loss', 'grad_x': 'grad_x', 'grad_mix_pre_g': 'grad_w', 'grad_mix_post_g': 'grad_w', 'grad_w_in': 'grad_w', 'grad_b_forget': 'grad_w', 'grad_pool_w': 'grad_w', 'grad_pool_scale': 'grad_w', 'grad_w_pool_br': 'grad_w', 'grad_w_fox_br': 'grad_w', 'grad_w_mix_out': 'grad_w', 'grad_xa_pre_g': 'grad_w', 'grad_xa_post_g': 'grad_w', 'grad_mem_g': 'grad_w', 'grad_w_xq': 'grad_w', 'grad_w_xkv': 'grad_w', 'grad_w_xo': 'grad_w', 'grad_ffn_pre_g': 'grad_w', 'grad_ffn_post_g': 'grad_w', 'grad_w_up': 'grad_w', 'grad_conv_w': 'grad_w', 'grad_conv_b': 'grad_w', 'grad_w_down': 'grad_w', 'delta_mix_pre_g': 'delta_w', 'delta_mix_post_g': 'delta_w', 'delta_w_in': 'delta_w', 'delta_b_forget': 'delta_w', 'delta_pool_w': 'delta_w', 'delta_pool_scale': 'delta_w', 'delta_w_pool_br': 'delta_w', 'delta_w_fox_br': 'delta_w', 'delta_w_mix_out': 'delta_w', 'delta_xa_pre_g': 'delta_w', 'delta_xa_post_g': 'delta_w', 'delta_mem_g': 'delta_w', 'delta_w_xq': 'delta_w', 'delta_w_xkv': 'delta_w', 'delta_w_xo': 'delta_w', 'delta_ffn_pre_g': 'delta_w', 'delta_ffn_post_g': 'delta_w', 'delta_w_up': 'delta_w', 'delta_conv_w': 'delta_w', 'delta_conv_b': 'delta_w', 'delta_w_down': 'delta_w', 'new_m_mix_pre_g': 'new_m', 'new_m_mix_post_g': 'new_m', 'new_m_w_in': 'new_m', 'new_m_b_forget': 'new_m', 'new_m_pool_w': 'new_m', 'new_m_pool_scale': 'new_m', 'new_m_w_pool_br': 'new_m', 'new_m_w_fox_br': 'new_m', 'new_m_w_mix_out': 'new_m', 'new_m_xa_pre_g': 'new_m', 'new_m_xa_post_g': 'new_m', 'new_m_mem_g': 'new_m', 'new_m_w_xq': 'new_m', 'new_m_w_xkv': 'new_m', 'new_m_w_xo': 'new_m', 'new_m_ffn_pre_g': 'new_m', 'new_m_ffn_post_g': 'new_m', 'new_m_w_up': 'new_m', 'new_m_conv_w': 'new_m', 'new_m_conv_b': 'new_m', 'new_m_w_down': 'new_m', 'new_v_mix_pre_g': 'new_v', 'new_v_mix_post_g': 'new_v', 'new_v_w_in': 'new_v', 'new_v_b_forget': 'new_v', 'new_v_pool_w': 'new_v', 'new_v_pool_scale': 'new_v', 'new_v_w_pool_br': 'new_v', 'new_v_w_fox_br': 'new_v', 'new_v_w_mix_out': 'new_v', 'new_v_xa_pre_g': 'new_v', 'new_v_xa_post_g': 'new_v', 'new_v_mem_g': 'new_v', 'new_v_w_xq': 'new_v', 'new_v_w_xkv': 'new_v', 'new_v_w_xo': 'new_v', 'new_v_ffn_pre_g': 'new_v', 'new_v_ffn_post_g': 'new_v', 'new_v_w_up': 'new_v', 'new_v_conv_w': 'new_v', 'new_v_conv_b': 'new_v', 'new_v_w_down': 'new_v'}


def _forward(args):
    return _fwd_reference(*[args[k] for k in FWD_PARAMS])


def _output_shape():
    out = _jax.eval_shape(lambda: _forward(_fwd_setup_inputs(0)))
    return out.shape, out.dtype

N_MICROBATCH = 1
ADAM_LR = 0.001
ADAM_B1 = 0.9
ADAM_B2 = 0.999
ADAM_EPS = 1e-08
ADAM_WD = 0.01
ADAM_STEP = 10
PER_EXAMPLE_BATCH_AXIS = {'x': 0, 'mem': 0, 'loss_target': 0}
SHARED_INPUTS = []
_WEIGHT_DTYPES = {'mix_pre_g': _jnp.float32, 'mix_post_g': _jnp.float32, 'w_in': _jnp.float32, 'b_forget': _jnp.float32, 'pool_w': _jnp.float32, 'pool_scale': _jnp.float32, 'w_pool_br': _jnp.float32, 'w_fox_br': _jnp.float32, 'w_mix_out': _jnp.float32, 'xa_pre_g': _jnp.float32, 'xa_post_g': _jnp.float32, 'mem_g': _jnp.float32, 'w_xq': _jnp.float32, 'w_xkv': _jnp.float32, 'w_xo': _jnp.float32, 'ffn_pre_g': _jnp.float32, 'ffn_post_g': _jnp.float32, 'w_up': _jnp.float32, 'conv_w': _jnp.float32, 'conv_b': _jnp.float32, 'w_down': _jnp.float32}
MOMENT_SCALE = {'mix_pre_g': 1.131890e+01, 'mix_post_g': 6.397783e+01, 'w_in': 5.546828e+00, 'b_forget': 4.291644e+00, 'pool_w': 4.521364e+00, 'pool_scale': 4.829946e+00, 'w_pool_br': 3.367712e+00, 'w_fox_br': 1.344993e+01, 'w_mix_out': 1.174261e+01, 'xa_pre_g': 6.738371e+00, 'xa_post_g': 7.389901e+01, 'mem_g': 2.895981e+01, 'w_xq': 9.191586e+00, 'w_xkv': 2.914777e+01, 'w_xo': 3.004747e+01, 'ffn_pre_g': 8.690749e+00, 'ffn_post_g': 6.370698e+01, 'w_up': 3.635007e+00, 'conv_w': 4.359401e+00, 'conv_b': 1.298404e+01, 'w_down': 8.014738e+00}


def _to_microbatches(a, axis):
    t = _jnp.moveaxis(a, axis, 0)
    t = t.reshape((N_MICROBATCH, t.shape[0] // N_MICROBATCH) + t.shape[1:])
    return _jnp.moveaxis(t, 1, axis + 1)


def setup_inputs(seed: int = 0) -> dict:
    inp = _fwd_setup_inputs(seed)
    key = _jax.random.fold_in(_jax.random.key(seed), 7919)
    shape, _ = _output_shape()
    out = dict(inp)
    out["loss_target"] = _jax.random.normal(_jax.random.fold_in(key, 0), shape, _jnp.float32)
    for i, name in enumerate(TWIN_WEIGHTS):
        w = inp[name].astype(_jnp.float32)
        if MOMENT_SCALE is None:
            s = _jnp.sqrt(_jnp.mean(_jnp.square(w)) + 1e-30)
        else:
            s = MOMENT_SCALE[name]
        km, kv = _jax.random.split(_jax.random.fold_in(key, i + 1))
        out[name] = w
        out["m_" + name] = s * _jax.random.normal(km, w.shape, _jnp.float32)
        out["v_" + name] = (s * s) * _jax.random.uniform(kv, w.shape, _jnp.float32, 0.5, 1.5)
    if N_MICROBATCH > 1:
        for name, axis in PER_EXAMPLE_BATCH_AXIS.items():
            out[name] = _to_microbatches(out[name], axis)
    return {'x': out['x'], 'mem': out['mem'], 'mix_pre_g': out['mix_pre_g'], 'mix_post_g': out['mix_post_g'], 'w_in': out['w_in'], 'b_forget': out['b_forget'], 'pool_w': out['pool_w'], 'pool_scale': out['pool_scale'], 'w_pool_br': out['w_pool_br'], 'w_fox_br': out['w_fox_br'], 'w_mix_out': out['w_mix_out'], 'xa_pre_g': out['xa_pre_g'], 'xa_post_g': out['xa_post_g'], 'mem_g': out['mem_g'], 'w_xq': out['w_xq'], 'w_xkv': out['w_xkv'], 'w_xo': out['w_xo'], 'ffn_pre_g': out['ffn_pre_g'], 'ffn_post_g': out['ffn_post_g'], 'w_up': out['w_up'], 'conv_w': out['conv_w'], 'conv_b': out['conv_b'], 'w_down': out['w_down'], 'loss_target': out['loss_target'], 'm_mix_pre_g': out['m_mix_pre_g'], 'm_mix_post_g': out['m_mix_post_g'], 'm_w_in': out['m_w_in'], 'm_b_forget': out['m_b_forget'], 'm_pool_w': out['m_pool_w'], 'm_pool_scale': out['m_pool_scale'], 'm_w_pool_br': out['m_w_pool_br'], 'm_w_fox_br': out['m_w_fox_br'], 'm_w_mix_out': out['m_w_mix_out'], 'm_xa_pre_g': out['m_xa_pre_g'], 'm_xa_post_g': out['m_xa_post_g'], 'm_mem_g': out['m_mem_g'], 'm_w_xq': out['m_w_xq'], 'm_w_xkv': out['m_w_xkv'], 'm_w_xo': out['m_w_xo'], 'm_ffn_pre_g': out['m_ffn_pre_g'], 'm_ffn_post_g': out['m_ffn_post_g'], 'm_w_up': out['m_w_up'], 'm_conv_w': out['m_conv_w'], 'm_conv_b': out['m_conv_b'], 'm_w_down': out['m_w_down'], 'v_mix_pre_g': out['v_mix_pre_g'], 'v_mix_post_g': out['v_mix_post_g'], 'v_w_in': out['v_w_in'], 'v_b_forget': out['v_b_forget'], 'v_pool_w': out['v_pool_w'], 'v_pool_scale': out['v_pool_scale'], 'v_w_pool_br': out['v_w_pool_br'], 'v_w_fox_br': out['v_w_fox_br'], 'v_w_mix_out': out['v_w_mix_out'], 'v_xa_pre_g': out['v_xa_pre_g'], 'v_xa_post_g': out['v_xa_post_g'], 'v_mem_g': out['v_mem_g'], 'v_w_xq': out['v_w_xq'], 'v_w_xkv': out['v_w_xkv'], 'v_w_xo': out['v_w_xo'], 'v_ffn_pre_g': out['v_ffn_pre_g'], 'v_ffn_post_g': out['v_ffn_post_g'], 'v_w_up': out['v_w_up'], 'v_conv_w': out['v_conv_w'], 'v_conv_b': out['v_conv_b'], 'v_w_down': out['v_w_down']}


def _loss(weights, diff, rest, loss_target):
    with _jax.named_scope("forward"):
        args = {**rest, TWIN_DIFF_INPUT: diff, **{k: w.astype(_WEIGHT_DTYPES[k]) for k, w in weights.items()}}
        y = _forward(args)
    with _jax.named_scope("loss_head"):
        err = _jnp.square(y.astype(_jnp.float32) - loss_target)
        return 0.5 * _jnp.sum(_jnp.mean(err, axis=-1)) if err.ndim else 0.5 * err


def _adamw(w, g, m, v):
    m = ADAM_B1 * m + (1.0 - ADAM_B1) * g
    v = ADAM_B2 * v + (1.0 - ADAM_B2) * _jnp.square(g)
    m_hat = m / (1.0 - ADAM_B1 ** ADAM_STEP)
    v_hat = v / (1.0 - ADAM_B2 ** ADAM_STEP)
    delta = -ADAM_LR * (m_hat / (_jnp.sqrt(v_hat) + ADAM_EPS) + ADAM_WD * w)
    return delta, m, v


def reference(x, mem, mix_pre_g, mix_post_g, w_in, b_forget, pool_w, pool_scale, w_pool_br, w_fox_br, w_mix_out, xa_pre_g, xa_post_g, mem_g, w_xq, w_xkv, w_xo, ffn_pre_g, ffn_post_g, w_up, conv_w, conv_b, w_down, loss_target, m_mix_pre_g, m_mix_post_g, m_w_in, m_b_forget, m_pool_w, m_pool_scale, m_w_pool_br, m_w_fox_br, m_w_mix_out, m_xa_pre_g, m_xa_post_g, m_mem_g, m_w_xq, m_w_xkv, m_w_xo, m_ffn_pre_g, m_ffn_post_g, m_w_up, m_conv_w, m_conv_b, m_w_down, v_mix_pre_g, v_mix_post_g, v_w_in, v_b_forget, v_pool_w, v_pool_scale, v_w_pool_br, v_w_fox_br, v_w_mix_out, v_xa_pre_g, v_xa_post_g, v_mem_g, v_w_xq, v_w_xkv, v_w_xo, v_ffn_pre_g, v_ffn_post_g, v_w_up, v_conv_w, v_conv_b, v_w_down):
    given = dict(x=x, mem=mem, mix_pre_g=mix_pre_g, mix_post_g=mix_post_g, w_in=w_in, b_forget=b_forget, pool_w=pool_w, pool_scale=pool_scale, w_pool_br=w_pool_br, w_fox_br=w_fox_br, w_mix_out=w_mix_out, xa_pre_g=xa_pre_g, xa_post_g=xa_post_g, mem_g=mem_g, w_xq=w_xq, w_xkv=w_xkv, w_xo=w_xo, ffn_pre_g=ffn_pre_g, ffn_post_g=ffn_post_g, w_up=w_up, conv_w=conv_w, conv_b=conv_b, w_down=w_down, loss_target=loss_target, m_mix_pre_g=m_mix_pre_g, m_mix_post_g=m_mix_post_g, m_w_in=m_w_in, m_b_forget=m_b_forget, m_pool_w=m_pool_w, m_pool_scale=m_pool_scale, m_w_pool_br=m_w_pool_br, m_w_fox_br=m_w_fox_br, m_w_mix_out=m_w_mix_out, m_xa_pre_g=m_xa_pre_g, m_xa_post_g=m_xa_post_g, m_mem_g=m_mem_g, m_w_xq=m_w_xq, m_w_xkv=m_w_xkv, m_w_xo=m_w_xo, m_ffn_pre_g=m_ffn_pre_g, m_ffn_post_g=m_ffn_post_g, m_w_up=m_w_up, m_conv_w=m_conv_w, m_conv_b=m_conv_b, m_w_down=m_w_down, v_mix_pre_g=v_mix_pre_g, v_mix_post_g=v_mix_post_g, v_w_in=v_w_in, v_b_forget=v_b_forget, v_pool_w=v_pool_w, v_pool_scale=v_pool_scale, v_w_pool_br=v_w_pool_br, v_w_fox_br=v_w_fox_br, v_w_mix_out=v_w_mix_out, v_xa_pre_g=v_xa_pre_g, v_xa_post_g=v_xa_post_g, v_mem_g=v_mem_g, v_w_xq=v_w_xq, v_w_xkv=v_w_xkv, v_w_xo=v_w_xo, v_ffn_pre_g=v_ffn_pre_g, v_ffn_post_g=v_ffn_post_g, v_w_up=v_w_up, v_conv_w=v_conv_w, v_conv_b=v_conv_b, v_w_down=v_w_down)
    weights = {n: given[n] for n in TWIN_WEIGHTS}
    shared = {n: given[n] for n in SHARED_INPUTS}
    per_example = {n: given[n] for n in ['x', 'mem']}
    grad_fn = _jax.value_and_grad(_loss, argnums=(0, 1))

    def one_microbatch(ex, loss_target):
        ex = dict(ex)
        diff = ex.pop(TWIN_DIFF_INPUT)
        return grad_fn(weights, diff, {**shared, **ex}, loss_target)

    if N_MICROBATCH == 1:
        loss, (grad_w, grad_x) = one_microbatch(per_example, given["loss_target"])
    else:
        def body(carry, xs):
            loss_sum, grad_sum = carry
            l_k, (gw_k, gx_k) = one_microbatch(xs[0], xs[1])
            with _jax.named_scope("update"):
                return (loss_sum + l_k, _jax.tree.map(_jnp.add, grad_sum, gw_k)), gx_k

        init = (_jnp.zeros((), _jnp.float32), _jax.tree.map(_jnp.zeros_like, weights))
        (loss, grad_w), grad_x = _jax.lax.scan(body, init, (per_example, given["loss_target"]))
    with _jax.named_scope("update"):
        delta_w, new_m, new_v = {}, {}, {}
        for n in TWIN_WEIGHTS:
            delta_w[n], new_m[n], new_v[n] = _adamw(weights[n], grad_w[n], given["m_" + n], given["v_" + n])
    return (loss, grad_x, *[grad_w[n] for n in TWIN_WEIGHTS], *[delta_w[n] for n in TWIN_WEIGHTS],
            *[new_m[n] for n in TWIN_WEIGHTS], *[new_v[n] for n in TWIN_WEIGHTS])
```

```python
import functools
import math

import jax
import jax.numpy as jnp
from jax import lax
from jax.experimental import pallas as pl
from jax.experimental.pallas import tpu as pltpu

F32 = jnp.float32
BF16 = jnp.bfloat16
MESH_AXES = ("x", "y", "c")
N_DEV = 8
MESH_ID = pl.DeviceIdType.MESH

DEPTH = 4
POOL_WINDOWS = (2, 4, 8, 16)
POOL_GROUP = 128
POOL_W = 512
POOL_HALO = 16
FOX_HEADS = 8
FOX_DH = 64
FOX_W = 512
X_HEADS = 4
X_DH = 128
X_W = 512
D_FF = 2816
RMS_EPS = 1e-6
ADAM_LR, ADAM_B1, ADAM_B2, ADAM_EPS, ADAM_WD, ADAM_STEP = 0.001, 0.9, 0.999, 1e-08, 0.01, 10

OFF_Q, OFF_F, OFF_GP, OFF_GF, IN_W = 512, 2048, 2056, 3080, 4104
ZR_GP, ZR_GF, ZR_U, ZR_F, ZR_W = 0, 1024, 2048, 2560, 3072
QKV_W = 3 * FOX_W
ZC_W = QKV_W + ZR_W

LANES = 1024
FLAT_ROWS = 128

SHARDED = ("w_in", "w_pool_br", "w_fox_br", "w_mix_out", "w_xq", "w_xkv", "w_xo", "w_up",
           "conv_w", "w_down")
COL_SHARDED = ("w_in", "w_pool_br", "w_fox_br", "w_xo", "w_up", "conv_w")
REPLICATED = ("mix_pre_g", "mix_post_g", "b_forget", "pool_w", "pool_scale", "xa_pre_g",
              "xa_post_g", "mem_g", "ffn_pre_g", "ffn_post_g", "conv_b")
WEIGHTS = ("mix_pre_g", "mix_post_g", "w_in", "b_forget", "pool_w", "pool_scale", "w_pool_br",
           "w_fox_br", "w_mix_out", "xa_pre_g", "xa_post_g", "mem_g", "w_xq", "w_xkv", "w_xo",
           "ffn_pre_g", "ffn_post_g", "w_up", "conv_w", "conv_b", "w_down")
INPUTS = (("x", "mem") + WEIGHTS + ("loss_target",) + tuple("m_" + n for n in WEIGHTS)
          + tuple("v_" + n for n in WEIGHTS))


def _pick(n, prefs):
    for p in prefs:
        if n % p == 0:
            return p
    return n


def _ktile(k):
    if k <= 2816:
        return k
    return _pick(k, (2816, 2048, 1536, 1024, 512))


def _round_up(n, m):
    return (n + m - 1) // m * m


def _params(*sem):
    return pltpu.CompilerParams(dimension_semantics=sem)


_DIMS = {"nn": (((1,), (0,)), ((), ())), "nt": (((1,), (1,)), ((), ())),
         "tn": (((0,), (0,)), ((), ()))}


def _mm(a, b, mode, out_dtype, name):
    if mode == "nn":
        (m, k), (_, n) = a.shape, b.shape
    elif mode == "nt":
        (m, k), (n, _) = a.shape, b.shape
    else:
        (k, m), (_, n) = a.shape, b.shape
    tm = _pick(m, (1024, 512, 256, 128))
    tn = _pick(n, (512, 256, 128))
    tk = _ktile(k)
    nk = k // tk

    def body(a_ref, b_ref, o_ref, acc_ref):
        kk = pl.program_id(2)

        @pl.when(kk == 0)
        def _():
            acc_ref[...] = jnp.zeros_like(acc_ref)

        acc_ref[...] += lax.dot_general(a_ref[...].astype(BF16), b_ref[...].astype(BF16),
                                        _DIMS[mode], preferred_element_type=F32)

        @pl.when(kk == nk - 1)
        def _():
            o_ref[...] = acc_ref[...].astype(out_dtype)

    if mode == "tn":
        a_spec = pl.BlockSpec((tk, tm), lambda i, j, kk: (kk, i))
    else:
        a_spec = pl.BlockSpec((tm, tk), lambda i, j, kk: (i, kk))
    if mode == "nt":
        b_spec = pl.BlockSpec((tn, tk), lambda i, j, kk: (j, kk))
    else:
        b_spec = pl.BlockSpec((tk, tn), lambda i, j, kk: (kk, j))
    return pl.pallas_call(
        body, name=name, grid=(m // tm, n // tn, nk),
        in_specs=[a_spec, b_spec],
        out_specs=pl.BlockSpec((tm, tn), lambda i, j, kk: (i, j)),
        out_shape=jax.ShapeDtypeStruct((m, n), out_dtype),
        scratch_shapes=[pltpu.VMEM((tm, tn), F32)],
        compiler_params=_params("parallel", "parallel", "arbitrary"),
    )(a, b)


def _rms(x, g):
    r = lax.rsqrt(jnp.mean(x * x, axis=-1, keepdims=True) + RMS_EPS)
    return x * r * g


def _rms_bwd(x, g, dy):
    r = lax.rsqrt(jnp.mean(x * x, axis=-1, keepdims=True) + RMS_EPS)
    xh = x * r
    t = dy * g
    dx = r * (t - xh * jnp.mean(t * xh, axis=-1, keepdims=True))
    dg = jnp.sum(dy * xh, axis=0, keepdims=True)
    return dx, dg


def _norm_fwd(x_in, f, g_post, g_pre, name):
    s, d = x_in.shape
    tm = _pick(s, (512, 256, 128))
    has_post, has_pre = f is not None, g_pre is not None
    row = pl.BlockSpec((tm, d), lambda i: (i, 0))
    vec = pl.BlockSpec((1, d), lambda i: (0, 0))

    def body(*refs):
        refs = list(refs)
        x = refs.pop(0)[...]
        if has_post:
            fv = refs.pop(0)[...]
            x = x + _rms(fv, refs.pop(0)[...])
        gpre = refs.pop(0)[...] if has_pre else None
        if has_post:
            refs.pop(0)[...] = x
        if has_pre:
            refs.pop(0)[...] = _rms(x, gpre).astype(BF16)

    ins, specs, outs, ospecs = [x_in], [row], [], []
    if has_post:
        ins += [f, g_post.reshape(1, d)]
        specs += [row, vec]
        outs.append(jax.ShapeDtypeStruct((s, d), F32))
        ospecs.append(row)
    if has_pre:
        ins.append(g_pre.reshape(1, d))
        specs.append(vec)
        outs.append(jax.ShapeDtypeStruct((s, d), BF16))
        ospecs.append(row)
    res = pl.pallas_call(body, name=name, grid=(s // tm,), in_specs=specs, out_specs=ospecs,
                         out_shape=outs, compiler_params=_params("parallel"))(*ins)
    res = list(res)
    x_out = res.pop(0) if has_post else None
    h = res.pop(0) if has_pre else None
    return x_out, h


def _norm_bwd(name, *, dh=None, x=None, g_pre=None, dres=None, y=None, tgt=None, f_prev=None,
              g_post=None):
    top = y is not None
    has_post = f_prev is not None
    ref_arr = y if top else x
    s, d = ref_arr.shape
    tm = _pick(s, (512, 256, 128))
    row = pl.BlockSpec((tm, d), lambda i: (i, 0))
    vec = pl.BlockSpec((1, d), lambda i: (0, 0))
    one = pl.BlockSpec((1, 1), lambda i: (0, 0))

    def body(*refs):
        refs = list(refs)
        i = pl.program_id(0)
        if top:
            yv, tv = refs.pop(0)[...], refs.pop(0)[...]
        else:
            dhv, xv, gv, dr = (refs.pop(0)[...].astype(F32), refs.pop(0)[...], refs.pop(0)[...],
                               refs.pop(0)[...])
        if has_post:
            fv, gp = refs.pop(0)[...], refs.pop(0)[...]
        dx_ref = refs.pop(0)
        if top:
            loss_ref = refs.pop(0)
        else:
            dgpre_ref = refs.pop(0)
        if has_post:
            df_ref, dgpost_ref = refs.pop(0), refs.pop(0)

        if top:
            err = yv - tv
            dx = err * (1.0 / d)
            part = 0.5 * jnp.sum(jnp.sum(err * err, axis=-1, keepdims=True) * (1.0 / d),
                                 axis=0, keepdims=True)
        else:
            dxn, dgpre = _rms_bwd(xv, gv, dhv)
            dx = dr + dxn
        dx_ref[...] = dx
        if has_post:
            df, dgpost = _rms_bwd(fv, gp, dx)
            df_ref[...] = df.astype(BF16)

        @pl.when(i == 0)
        def _():
            if top:
                loss_ref[...] = jnp.zeros_like(loss_ref)
            else:
                dgpre_ref[...] = jnp.zeros_like(dgpre_ref)
            if has_post:
                dgpost_ref[...] = jnp.zeros_like(dgpost_ref)

        if top:
            loss_ref[...] += part
        else:
            dgpre_ref[...] += dgpre
        if has_post:
            dgpost_ref[...] += dgpost

    if top:
        ins, specs = [y, tgt], [row, row]
    else:
        ins, specs = [dh, x, g_pre.reshape(1, d), dres], [row, row, vec, row]
    if has_post:
        ins += [f_prev, g_post.reshape(1, d)]
        specs += [row, vec]
    outs, ospecs = [jax.ShapeDtypeStruct((s, d), F32)], [row]
    if top:
        outs.append(jax.ShapeDtypeStruct((1, 1), F32))
        ospecs.append(one)
    else:
        outs.append(jax.ShapeDtypeStruct((1, d), F32))
        ospecs.append(vec)
    if has_post:
        outs += [jax.ShapeDtypeStruct((s, d), BF16), jax.ShapeDtypeStruct((1, d), F32)]
        ospecs += [row, vec]
    res = list(pl.pallas_call(body, name=name, grid=(s // tm,), in_specs=specs,
                              out_specs=ospecs, out_shape=outs,
                              compiler_params=_params("arbitrary"))(*ins))
    out = {"dx": res.pop(0)}
    out["loss" if top else "dg_pre"] = res.pop(0)
    if has_post:
        out["df"], out["dg_post"] = res.pop(0), res.pop(0)
    return out


def _rms_dg(x, g, dy, name):
    s, d = x.shape
    tm = _pick(s, (256, 128))
    row = pl.BlockSpec((tm, d), lambda i: (i, 0))
    vec = pl.BlockSpec((1, d), lambda i: (0, 0))

    def body(x_ref, g_ref, dy_ref, dg_ref):
        @pl.when(pl.program_id(0) == 0)
        def _():
            dg_ref[...] = jnp.zeros_like(dg_ref)

        dg_ref[...] += _rms_bwd(x_ref[...], g_ref[...], dy_ref[...])[1]

    return pl.pallas_call(body, name=name, grid=(s // tm,), in_specs=[row, vec, row],
                          out_specs=vec, out_shape=jax.ShapeDtypeStruct((1, d), F32),
                          compiler_params=_params("arbitrary"))(x, g.reshape(1, d), dy)


def _forget_fwd(zr, b128, name):
    s = zr.shape[0]
    tm = _pick(s, (256, 128))
    fblk = ZR_F // 128

    def body(z_ref, b_ref, c_ref, carry):
        @pl.when(pl.program_id(0) == 0)
        def _():
            carry[...] = jnp.zeros_like(carry)

        a = z_ref[...] + b_ref[...]
        acc = jnp.minimum(a, 0.0) - jnp.log1p(jnp.exp(-jnp.abs(a)))
        rows = lax.broadcasted_iota(jnp.int32, acc.shape, 0)
        k = 1
        while k < tm:
            acc = acc + jnp.where(rows >= k, pltpu.roll(acc, k, 0), 0.0)
            k *= 2
        acc = acc + carry[...]
        c_ref[...] = acc
        carry[...] = acc[tm - 1:tm, :]

    return pl.pallas_call(
        body, name=name, grid=(s // tm,),
        in_specs=[pl.BlockSpec((tm, 128), lambda i: (i, fblk)),
                  pl.BlockSpec((1, 128), lambda i: (0, 0))],
        out_specs=pl.BlockSpec((tm, 128), lambda i: (i, 0)),
        out_shape=jax.ShapeDtypeStruct((s, 128), F32),
        scratch_shapes=[pltpu.VMEM((1, 128), F32)],
        compiler_params=_params("arbitrary"))(zr, b128)


def _forget_bwd(dc, zr, b128, name):
    s = zr.shape[0]
    tm = _pick(s, (256, 128))
    nt = s // tm
    fblk = ZR_F // 128

    def body(dc_ref, z_ref, b_ref, dz_ref, db_ref, carry):
        @pl.when(pl.program_id(0) == 0)
        def _():
            carry[...] = jnp.zeros_like(carry)
            db_ref[...] = jnp.zeros_like(db_ref)

        acc = dc_ref[...]
        rows = lax.broadcasted_iota(jnp.int32, acc.shape, 0)
        k = 1
        while k < tm:
            acc = acc + jnp.where(rows < tm - k, pltpu.roll(acc, tm - k, 0), 0.0)
            k *= 2
        acc = acc + carry[...]
        carry[...] = acc[0:1, :]
        a = z_ref[...] + b_ref[...]
        dz = acc / (1.0 + jnp.exp(a))
        dz_ref[...] = dz
        db_ref[...] += jnp.sum(dz, axis=0, keepdims=True)

    return pl.pallas_call(
        body, name=name, grid=(nt,),
        in_specs=[pl.BlockSpec((tm, 128), lambda i: (nt - 1 - i, 0)),
                  pl.BlockSpec((tm, 128), lambda i: (nt - 1 - i, fblk)),
                  pl.BlockSpec((1, 128), lambda i: (0, 0))],
        out_specs=[pl.BlockSpec((tm, 128), lambda i: (nt - 1 - i, 0)),
                   pl.BlockSpec((1, 128), lambda i: (0, 0))],
        out_shape=[jax.ShapeDtypeStruct((s, 128), F32), jax.ShapeDtypeStruct((1, 128), F32)],
        scratch_shapes=[pltpu.VMEM((1, 128), F32)],
        compiler_params=_params("arbitrary"))(dc, zr, b128)


def _pooled(ext, t_abs, g, w):
    e = ext[:, g * POOL_GROUP:(g + 1) * POOL_GROUP]
    acc = e
    k = 1
    while k < w:
        acc = acc + pltpu.roll(acc, k, 0)
        k *= 2
    cnt = jnp.minimum(t_abs + 1, w).astype(F32)
    return acc[POOL_HALO:] / cnt - e[POOL_HALO:]


def _pool_specs(s, tm):
    per = tm // POOL_HALO
    ublk = ZR_U // POOL_W
    return [pl.BlockSpec((tm, POOL_W), lambda i: (i, ublk)),
            pl.BlockSpec((POOL_HALO, POOL_W), lambda i: (jnp.maximum(i * per - 1, 0), ublk))]


def _pool_fwd(zr, pw, scale, name):
    s = zr.shape[0]
    tm = _pick(s, (512, 256, 128))

    def body(u_ref, h_ref, pw_ref, sc_ref, o_ref):
        i = pl.program_id(0)
        halo = jnp.where(i > 0, h_ref[...], 0.0)
        ext = jnp.concatenate([halo, u_ref[...]], axis=0)
        t_abs = i * tm + lax.broadcasted_iota(jnp.int32, (tm, 1), 0)
        outs = []
        for g, w in enumerate(POOL_WINDOWS):
            pooled = _pooled(ext, t_abs, g, w)
            outs.append(jnp.dot(pooled.astype(BF16), pw_ref[g], preferred_element_type=F32))
        o_ref[...] = (jnp.concatenate(outs, axis=1) * sc_ref[...]).astype(BF16)

    return pl.pallas_call(
        body, name=name, grid=(s // tm,),
        in_specs=_pool_specs(s, tm) + [
            pl.BlockSpec((len(POOL_WINDOWS), POOL_GROUP, POOL_GROUP), lambda i: (0, 0, 0)),
            pl.BlockSpec((1, POOL_W), lambda i: (0, 0))],
        out_specs=pl.BlockSpec((tm, POOL_W), lambda i: (i, 0)),
        out_shape=jax.ShapeDtypeStruct((s, POOL_W), BF16),
        compiler_params=_params("parallel"))(zr, zr, pw, scale)


def _pool_bwd(zr, dpm, pw, scale, name):
    s = zr.shape[0]
    tm = _pick(s, (512, 256, 128))
    nt = s // tm
    per = tm // POOL_HALO
    n_ext = tm + POOL_HALO
    ng = len(POOL_WINDOWS)

    def body(u_ref, h_ref, d_ref, dn_ref, pw_ref, sc_ref, du_ref, dpw_ref, dsc_ref):
        i = pl.program_id(0)

        @pl.when(i == 0)
        def _():
            dpw_ref[...] = jnp.zeros_like(dpw_ref)
            dsc_ref[...] = jnp.zeros_like(dsc_ref)

        halo = jnp.where(i > 0, h_ref[...], 0.0)
        ext_u = jnp.concatenate([halo, u_ref[...]], axis=0)
        nxt = jnp.where(i < nt - 1, dn_ref[...], 0.0)
        ext_d = jnp.concatenate([d_ref[...], nxt], axis=0)
        t_abs = i * tm + lax.broadcasted_iota(jnp.int32, (tm, 1), 0)
        t_ext = i * tm + lax.broadcasted_iota(jnp.int32, (n_ext, 1), 0)
        dus, dscs = [], []
        for g, w in enumerate(POOL_WINDOWS):
            sl = slice(g * POOL_GROUP, (g + 1) * POOL_GROUP)
            pooled = _pooled(ext_u, t_abs, g, w).astype(BF16)
            mixed = jnp.dot(pooled, pw_ref[g], preferred_element_type=F32)
            d_g = ext_d[:, sl]
            dscs.append(jnp.sum(d_g[:tm] * mixed, axis=0, keepdims=True))
            dmixed = (d_g * sc_ref[:, sl]).astype(BF16)
            dpw_ref[g] += lax.dot_general(pooled, dmixed[:tm], _DIMS["tn"],
                                          preferred_element_type=F32)
            dpooled = lax.dot_general(dmixed, pw_ref[g], _DIMS["nt"], preferred_element_type=F32)
            acc = dpooled / jnp.minimum(t_ext + 1, w).astype(F32)
            k = 1
            while k < w:
                acc = acc + pltpu.roll(acc, n_ext - k, 0)
                k *= 2
            dus.append(acc[:tm] - dpooled[:tm])
        du_ref[...] = jnp.concatenate(dus, axis=1).astype(BF16)
        dsc_ref[...] += jnp.concatenate(dscs, axis=1)

    return pl.pallas_call(
        body, name=name, grid=(nt,),
        in_specs=_pool_specs(s, tm) + [
            pl.BlockSpec((tm, POOL_W), lambda i: (i, 0)),
            pl.BlockSpec((POOL_HALO, POOL_W),
                         lambda i: (jnp.minimum((i + 1) * per, s // POOL_HALO - 1), 0)),
            pl.BlockSpec((ng, POOL_GROUP, POOL_GROUP), lambda i: (0, 0, 0)),
            pl.BlockSpec((1, POOL_W), lambda i: (0, 0))],
        out_specs=[pl.BlockSpec((tm, POOL_W), lambda i: (i, 0)),
                   pl.BlockSpec((ng, POOL_GROUP, POOL_GROUP), lambda i: (0, 0, 0)),
                   pl.BlockSpec((1, POOL_W), lambda i: (0, 0))],
        out_shape=[jax.ShapeDtypeStruct((s, POOL_W), BF16),
                   jax.ShapeDtypeStruct((ng, POOL_GROUP, POOL_GROUP), F32),
                   jax.ShapeDtypeStruct((1, POOL_W), F32)],
        compiler_params=_params("arbitrary"))(zr, zr, dpm, dpm, pw, scale)


def _fox_logits(q, k, cq, ck, i, j, tq, tk, scale):
    sc = lax.dot_general(q, k, _DIMS["nt"], preferred_element_type=F32) * scale + cq - ck
    rows = i * tq + lax.broadcasted_iota(jnp.int32, (tq, tk), 0)
    cols = j * tk + lax.broadcasted_iota(jnp.int32, (tq, tk), 1)
    return jnp.where(cols <= rows, sc, -jnp.inf)


def _fox_fwd(q, k, v, c_col, c_row, name):
    h, s, dh = q.shape
    t = _pick(s, (512, 256, 128))
    nb = s // t
    scale = 1.0 / math.sqrt(dh)

    def body(q_ref, k_ref, v_ref, cq_ref, ck_ref, o_ref, lse_ref, m_sc, l_sc, acc_sc):
        i, j = pl.program_id(1), pl.program_id(2)

        @pl.when(j == 0)
        def _():
            m_sc[...] = jnp.full_like(m_sc, -jnp.inf)
            l_sc[...] = jnp.zeros_like(l_sc)
            acc_sc[...] = jnp.zeros_like(acc_sc)

        @pl.when(j <= i)
        def _():
            sc = _fox_logits(q_ref[...], k_ref[...], cq_ref[...], ck_ref[...], i, j, t, t, scale)
            m_new = jnp.maximum(m_sc[...], jnp.max(sc, axis=-1, keepdims=True))
            alpha = jnp.exp(m_sc[...] - m_new)
            p = jnp.exp(sc - m_new)
            l_sc[...] = alpha * l_sc[...] + jnp.sum(p, axis=-1, keepdims=True)
            p_hi = p.astype(BF16)
            p_lo = (p - p_hi.astype(F32)).astype(BF16)
            vv = v_ref[...]
            acc_sc[...] = (alpha * acc_sc[...] + jnp.dot(p_hi, vv, preferred_element_type=F32)
                           + jnp.dot(p_lo, vv, preferred_element_type=F32))
            m_sc[...] = m_new

        @pl.when(j == nb - 1)
        def _():
            o_ref[...] = acc_sc[...] / l_sc[...]
            lse_ref[...] = m_sc[...] + jnp.log(l_sc[...])

    qspec = pl.BlockSpec((None, t, dh), lambda hh, i, j: (hh, i, 0))
    kspec = pl.BlockSpec((None, t, dh), lambda hh, i, j: (hh, jnp.minimum(j, i), 0))
    colspec = pl.BlockSpec((None, t, 1), lambda hh, i, j: (hh, i, 0))
    return pl.pallas_call(
        body, name=name, grid=(h, nb, nb),
        in_specs=[qspec, kspec, kspec, colspec,
                  pl.BlockSpec((None, 1, t), lambda hh, i, j: (hh, 0, jnp.minimum(j, i)))],
        out_specs=[qspec, colspec],
        out_shape=[jax.ShapeDtypeStruct((h, s, dh), F32), jax.ShapeDtypeStruct((h, s, 1), F32)],
        scratch_shapes=[pltpu.VMEM((t, 1), F32), pltpu.VMEM((t, 1), F32),
                        pltpu.VMEM((t, dh), F32)],
        compiler_params=_params("parallel", "parallel", "arbitrary"))(q, k, v, c_col, c_row)


def _fox_bwd(q, k, v, c_col, c_row, o, lse, do, name):
    h, s, dh = q.shape
    t = _pick(s, (512, 256, 128))
    nb = s // t
    scale = 1.0 / math.sqrt(dh)

    def body(q_ref, k_ref, v_ref, cq_ref, ck_ref, o_ref, lse_ref, do_ref,
             dq_ref, dk_ref, dv_ref, dc_ref, dk_sc, dv_sc, dc_sc):
        j, i = pl.program_id(1), pl.program_id(2)

        @pl.when((j == 0) & (i == 0))
        def _():
            dq_ref[...] = jnp.zeros_like(dq_ref)

        @pl.when(i == 0)
        def _():
            dk_sc[...] = jnp.zeros_like(dk_sc)
            dv_sc[...] = jnp.zeros_like(dv_sc)
            dc_sc[...] = jnp.zeros_like(dc_sc)

        @pl.when(i >= j)
        def _():
            qv, kv, dov = q_ref[...], k_ref[...], do_ref[...]
            sc = _fox_logits(qv, kv, cq_ref[...], ck_ref[...], i, j, t, t, scale)
            p = jnp.exp(sc - lse_ref[...])
            delta = jnp.sum(dov.astype(F32) * o_ref[...], axis=-1, keepdims=True)
            dv_sc[...] += lax.dot_general(p.astype(BF16), dov, _DIMS["tn"],
                                          preferred_element_type=F32)
            dp = lax.dot_general(dov, v_ref[...], _DIMS["nt"], preferred_element_type=F32)
            ds = p * (dp - delta)
            dc_sc[...] += jnp.sum(ds, axis=0, keepdims=True)
            dsb = (ds * scale).astype(BF16)
            dk_sc[...] += lax.dot_general(dsb, qv, _DIMS["tn"], preferred_element_type=F32)
            rows = pl.ds(pl.multiple_of(i * t, t), t)
            dq_ref[rows, :] += jnp.dot(dsb, kv, preferred_element_type=F32)

        @pl.when(i == nb - 1)
        def _():
            dk_ref[...] = dk_sc[...]
            dv_ref[...] = dv_sc[...]
            dc_ref[...] = -dc_sc[...]

    qspec = pl.BlockSpec((None, t, dh), lambda hh, j, i: (hh, jnp.maximum(i, j), 0))
    kspec = pl.BlockSpec((None, t, dh), lambda hh, j, i: (hh, j, 0))
    qcol = pl.BlockSpec((None, t, 1), lambda hh, j, i: (hh, jnp.maximum(i, j), 0))
    krow = pl.BlockSpec((None, 1, t), lambda hh, j, i: (hh, 0, j))
    return pl.pallas_call(
        body, name=name, grid=(h, nb, nb),
        in_specs=[qspec, kspec, kspec, qcol, krow, qspec, qcol, qspec],
        out_specs=[pl.BlockSpec((None, s, dh), lambda hh, j, i: (hh, 0, 0)), kspec, kspec, krow],
        out_shape=[jax.ShapeDtypeStruct((h, s, dh), F32), jax.ShapeDtypeStruct((h, s, dh), F32),
                   jax.ShapeDtypeStruct((h, s, dh), F32), jax.ShapeDtypeStruct((h, 1, s), F32)],
        scratch_shapes=[pltpu.VMEM((t, dh), F32), pltpu.VMEM((t, dh), F32),
                        pltpu.VMEM((1, t), F32)],
        compiler_params=_params("parallel", "arbitrary", "arbitrary"))(
            q, k, v, c_col, c_row, o, lse, do)


def _gate_fwd(zr, yp, yf, name):
    s, d = yp.shape
    tm = _pick(s, (256, 128))
    row = pl.BlockSpec((tm, d), lambda i: (i, 0))

    def body(zp_ref, zf_ref, yp_ref, yf_ref, o_ref):
        o_ref[...] = (jax.nn.sigmoid(zp_ref[...]) * yp_ref[...]
                      + jax.nn.sigmoid(zf_ref[...]) * yf_ref[...]).astype(BF16)

    return pl.pallas_call(
        body, name=name, grid=(s // tm,),
        in_specs=[pl.BlockSpec((tm, d), lambda i: (i, ZR_GP // d)),
                  pl.BlockSpec((tm, d), lambda i: (i, ZR_GF // d)), row, row],
        out_specs=row, out_shape=jax.ShapeDtypeStruct((s, d), BF16),
        compiler_params=_params("parallel"))(zr, zr, yp, yf)


def _gate_bwd(dm, zr, yp, yf, name):
    s, d = yp.shape
    tm = _pick(s, (256, 128))
    row = pl.BlockSpec((tm, d), lambda i: (i, 0))

    def body(dm_ref, zp_ref, zf_ref, yp_ref, yf_ref, dyp_ref, dyf_ref, dz_ref):
        dmv = dm_ref[...]
        gp, gf = jax.nn.sigmoid(zp_ref[...]), jax.nn.sigmoid(zf_ref[...])
        dyp_ref[...] = (dmv * gp).astype(BF16)
        dyf_ref[...] = (dmv * gf).astype(BF16)
        dz_ref[:, :d] = (dmv * yp_ref[...] * gp * (1.0 - gp)).astype(BF16)
        dz_ref[:, d:] = (dmv * yf_ref[...] * gf * (1.0 - gf)).astype(BF16)

    return pl.pallas_call(
        body, name=name, grid=(s // tm,),
        in_specs=[row, pl.BlockSpec((tm, d), lambda i: (i, ZR_GP // d)),
                  pl.BlockSpec((tm, d), lambda i: (i, ZR_GF // d)), row, row],
        out_specs=[row, row, pl.BlockSpec((tm, 2 * d), lambda i: (i, 0))],
        out_shape=[jax.ShapeDtypeStruct((s, d), BF16), jax.ShapeDtypeStruct((s, d), BF16),
                   jax.ShapeDtypeStruct((s, 2 * d), BF16)],
        compiler_params=_params("parallel"))(dm, zr, zr, yp, yf)


def _xattn_probs(q, k):
    sc = lax.dot_general(q, k, _DIMS["nt"], preferred_element_type=F32) * (1.0 / math.sqrt(X_DH))
    p = jnp.exp(sc - jnp.max(sc, axis=-1, keepdims=True))
    return p / jnp.sum(p, axis=-1, keepdims=True)


def _xattn_fwd(q, kv, name):
    s = q.shape[0]
    m = kv.shape[0]
    tq = _pick(s, (512, 256, 128))

    def body(q_ref, k_ref, v_ref, o_ref):
        p = _xattn_probs(q_ref[...], k_ref[...])
        o_ref[...] = jnp.dot(p.astype(BF16), v_ref[...],
                             preferred_element_type=F32).astype(BF16)

    qspec = pl.BlockSpec((tq, X_DH), lambda i, hh: (i, hh))
    return pl.pallas_call(
        body, name=name, grid=(s // tq, X_HEADS),
        in_specs=[qspec, pl.BlockSpec((m, X_DH), lambda i, hh: (0, hh)),
                  pl.BlockSpec((m, X_DH), lambda i, hh: (0, X_HEADS + hh))],
        out_specs=qspec, out_shape=jax.ShapeDtypeStruct((s, X_W), BF16),
        compiler_params=_params("parallel", "parallel"))(q, kv, kv)


def _xattn_bwd(q, kv, do, name):
    s = q.shape[0]
    m = kv.shape[0]
    tq = _pick(s, (512, 256, 128))
    scale = 1.0 / math.sqrt(X_DH)

    def body(q_ref, k_ref, v_ref, do_ref, dq_ref, dk_ref, dv_ref):
        @pl.when(pl.program_id(1) == 0)
        def _():
            dk_ref[...] = jnp.zeros_like(dk_ref)
            dv_ref[...] = jnp.zeros_like(dv_ref)

        qv, kk, dov = q_ref[...], k_ref[...], do_ref[...]
        p = _xattn_probs(qv, kk)
        dv_ref[...] += lax.dot_general(p.astype(BF16), dov, _DIMS["tn"],
                                       preferred_element_type=F32)
        dp = lax.dot_general(dov, v_ref[...], _DIMS["nt"], preferred_element_type=F32)
        ds = p * (dp - jnp.sum(dp * p, axis=-1, keepdims=True))
        dsb = (ds * scale).astype(BF16)
        dq_ref[...] = jnp.dot(dsb, kk, preferred_element_type=F32).astype(BF16)
        dk_ref[...] += lax.dot_general(dsb, qv, _DIMS["tn"], preferred_element_type=F32)

    qspec = pl.BlockSpec((tq, X_DH), lambda hh, i: (i, hh))
    kspec = pl.BlockSpec((m, X_DH), lambda hh, i: (0, hh))
    return pl.pallas_call(
        body, name=name, grid=(X_HEADS, s // tq),
        in_specs=[qspec, kspec, pl.BlockSpec((m, X_DH), lambda hh, i: (0, X_HEADS + hh)), qspec],
        out_specs=[qspec, kspec, kspec],
        out_shape=[jax.ShapeDtypeStruct((s, X_W), BF16), jax.ShapeDtypeStruct((m, X_W), F32),
                   jax.ShapeDtypeStruct((m, X_W), F32)],
        compiler_params=_params("parallel", "arbitrary"))(q, kv, kv, do)


_GELU_C = math.sqrt(2.0 / math.pi)
CONV_HALO = 8


def _gelu(x):
    return 0.5 * x * (1.0 + jnp.tanh(_GELU_C * (x + 0.044715 * x * x * x)))


def _gelu_grad(x):
    th = jnp.tanh(_GELU_C * (x + 0.044715 * x * x * x))
    return 0.5 * (1.0 + th) + 0.5 * x * (1.0 - th * th) * _GELU_C * (1.0 + 3 * 0.044715 * x * x)


def _shift_down(z, halo, n):
    rows = lax.broadcasted_iota(jnp.int32, z.shape, 0)
    out = pltpu.roll(z, n, 0)
    for r in range(n):
        out = jnp.where(rows == r, halo[CONV_HALO - n + r:CONV_HALO - n + r + 1], out)
    return out


def _conv(z, halo, cw, cb):
    return cw[2:3] * z + cw[1:2] * _shift_down(z, halo, 1) + cw[0:1] * _shift_down(z, halo, 2) + cb


def _conv_specs(tm, tn, off):
    per = tm // CONV_HALO
    return [pl.BlockSpec((tm, tn), lambda j, i: (i, j + off)),
            pl.BlockSpec((CONV_HALO, tn), lambda j, i: (jnp.maximum(i * per - 1, 0), j + off)),
            pl.BlockSpec((3, tn), lambda j, i: (0, j + off)),
            pl.BlockSpec((1, tn), lambda j, i: (0, j + off))]


def _convglu_fwd(z, cw, cb, name):
    s = z.shape[0]
    tm = _pick(s, (512, 256, 128))
    tn = 256
    nj = D_FF // tn

    def body(zg_ref, hg_ref, cwg_ref, cbg_ref, zu_ref, hu_ref, cwu_ref, cbu_ref, a_ref):
        first = pl.program_id(1) == 0
        gc = _conv(zg_ref[...], jnp.where(first, 0.0, hg_ref[...]), cwg_ref[...], cbg_ref[...])
        uc = _conv(zu_ref[...], jnp.where(first, 0.0, hu_ref[...]), cwu_ref[...], cbu_ref[...])
        a_ref[...] = (_gelu(gc) * uc).astype(BF16)

    return pl.pallas_call(
        body, name=name, grid=(nj, s // tm),
        in_specs=_conv_specs(tm, tn, 0) + _conv_specs(tm, tn, nj),
        out_specs=pl.BlockSpec((tm, tn), lambda j, i: (i, j)),
        out_shape=jax.ShapeDtypeStruct((s, D_FF), BF16),
        compiler_params=_params("parallel", "parallel"))(z, z, cw, cb, z, z, cw, cb)


def _convglu_bwd_pre(z, da, cw, cb, name):
    s = z.shape[0]
    tm = _pick(s, (512, 256, 128))
    tn = 256
    nj = D_FF // tn

    def body(zg_ref, hg_ref, cwg_ref, cbg_ref, zu_ref, hu_ref, cwu_ref, cbu_ref, da_ref,
             dg_ref, du_ref, dcwg_ref, dcwu_ref, dcbg_ref, dcbu_ref):
        first = pl.program_id(1) == 0

        @pl.when(first)
        def _():
            for r in (dcwg_ref, dcwu_ref, dcbg_ref, dcbu_ref):
                r[...] = jnp.zeros_like(r)

        dav = da_ref[...].astype(F32)
        zg, zu = zg_ref[...], zu_ref[...]
        hg = jnp.where(first, 0.0, hg_ref[...])
        hu = jnp.where(first, 0.0, hu_ref[...])
        gc = _conv(zg, hg, cwg_ref[...], cbg_ref[...])
        uc = _conv(zu, hu, cwu_ref[...], cbu_ref[...])
        dgc = dav * uc * _gelu_grad(gc)
        duc = dav * _gelu(gc)
        dg_ref[...] = dgc
        du_ref[...] = duc
        for d, zz, hh, dcw_ref, dcb_ref in ((dgc, zg, hg, dcwg_ref, dcbg_ref),
                                            (duc, zu, hu, dcwu_ref, dcbu_ref)):
            dcw_ref[...] += jnp.concatenate(
                [jnp.sum(d * _shift_down(zz, hh, 2), axis=0, keepdims=True),
                 jnp.sum(d * _shift_down(zz, hh, 1), axis=0, keepdims=True),
                 jnp.sum(d * zz, axis=0, keepdims=True)], axis=0)
            dcb_ref[...] += jnp.sum(d, axis=0, keepdims=True)

    tile = pl.BlockSpec((tm, tn), lambda j, i: (i, j))
    wspec = pl.BlockSpec((3, tn), lambda j, i: (0, j))
    bspec = pl.BlockSpec((1, tn), lambda j, i: (0, j))
    return pl.pallas_call(
        body, name=name, grid=(nj, s // tm),
        in_specs=_conv_specs(tm, tn, 0) + _conv_specs(tm, tn, nj) + [tile],
        out_specs=[tile, tile, wspec, wspec, bspec, bspec],
        out_shape=[jax.ShapeDtypeStruct((s, D_FF), F32), jax.ShapeDtypeStruct((s, D_FF), F32),
                   jax.ShapeDtypeStruct((3, D_FF), F32), jax.ShapeDtypeStruct((3, D_FF), F32),
                   jax.ShapeDtypeStruct((1, D_FF), F32), jax.ShapeDtypeStruct((1, D_FF), F32)],
        compiler_params=_params("parallel", "arbitrary"))(z, z, cw, cb, z, z, cw, cb, da)


def _convglu_bwd_post(dzc, cw, off, name):
    s, n = dzc.shape
    tm = _pick(s, (512, 256, 128))
    tn = 256
    nt = s // tm
    per = tm // CONV_HALO

    def body(d_ref, n_ref, cw_ref, o_ref):
        d = d_ref[...]
        nxt = jnp.where(pl.program_id(1) == nt - 1, 0.0, n_ref[...])
        rows = lax.broadcasted_iota(jnp.int32, d.shape, 0)
        acc = cw_ref[2:3] * d
        for k in (1, 2):
            up = pltpu.roll(d, tm - k, 0)
            for r in range(k):
                up = jnp.where(rows == tm - k + r, nxt[r:r + 1], up)
            acc = acc + cw_ref[2 - k:3 - k] * up
        o_ref[...] = acc.astype(BF16)

    return pl.pallas_call(
        body, name=name, grid=(n // tn, nt),
        in_specs=[pl.BlockSpec((tm, tn), lambda j, i: (i, j)),
                  pl.BlockSpec((CONV_HALO, tn),
                               lambda j, i: (jnp.minimum((i + 1) * per, s // CONV_HALO - 1), j)),
                  pl.BlockSpec((3, tn), lambda j, i: (0, j + off))],
        out_specs=pl.BlockSpec((tm, tn), lambda j, i: (i, j)),
        out_shape=jax.ShapeDtypeStruct((s, n), BF16),
        compiler_params=_params("parallel", "parallel"))(dzc, dzc, cw)


ANY = pl.BlockSpec(memory_space=pl.ANY)


def _place():
    x, y, c = (lax.axis_index(a) for a in MESH_AXES)
    return x, y, c, [(1 - x, y), (x, 1 - y), (1 - x, 1 - y)]


def _allgather(src, name):
    r, cdim = src.shape

    def body(x_ref, out_ref, send_sems, recv_sems, local_sem):
        x, y, c, chips = _place()
        me, sibling = (x, y, c), (x, y, 1 - c)

        def row(px, py, pc):
            return out_ref.at[4 * px + 2 * py + pc]

        def copy(k, block, to, src_ref=None):
            return pltpu.make_async_remote_copy(
                src_ref=row(*block) if src_ref is None else src_ref, dst_ref=row(*block),
                send_sem=send_sems.at[k], recv_sem=recv_sems.at[k], device_id=to,
                device_id_type=MESH_ID)

        mine = pltpu.make_async_copy(x_ref, row(*me), local_sem)
        mine.start()
        first = [copy(0, me, sibling, src_ref=x_ref)]
        first += [copy(1 + j, me, (*chip, c), src_ref=x_ref) for j, chip in enumerate(chips)]
        for cp in first:
            cp.start()
        passed = [copy(4 + j, (*chip, c), sibling) for j, chip in enumerate(chips)]
        for j, chip in enumerate(chips):
            copy(1 + j, (*chip, c), me).wait_recv()
            passed[j].start()
        copy(0, sibling, me).wait_recv()
        for j, chip in enumerate(chips):
            copy(4 + j, (*chip, 1 - c), me).wait_recv()
        for cp in first + passed:
            cp.wait_send()
        mine.wait()

    return pl.pallas_call(
        body, name=name, in_specs=[ANY], out_specs=ANY,
        out_shape=jax.ShapeDtypeStruct((N_DEV, r, cdim), src.dtype),
        scratch_shapes=[pltpu.SemaphoreType.DMA((7,)), pltpu.SemaphoreType.DMA((7,)),
                        pltpu.SemaphoreType.DMA(())],
    )(src)


def _swap_with_sibling(buf, name):
    _, nchip, r, cdim = buf.shape

    def body(g_ref, rcv_ref, send_sems, recv_sems):
        x, y, c, _ = _place()
        copies = [pltpu.make_async_remote_copy(
            src_ref=g_ref.at[1 - c, k], dst_ref=rcv_ref.at[k], send_sem=send_sems.at[k],
            recv_sem=recv_sems.at[k], device_id=(x, y, 1 - c), device_id_type=MESH_ID)
            for k in range(nchip)]
        for cp in copies:
            cp.start()
        for cp in copies:
            cp.wait()

    return pl.pallas_call(
        body, name=name, in_specs=[ANY], out_specs=ANY,
        out_shape=jax.ShapeDtypeStruct((nchip, r, cdim), buf.dtype),
        scratch_shapes=[pltpu.SemaphoreType.DMA((nchip,)), pltpu.SemaphoreType.DMA((nchip,))],
    )(buf)


def _exchange_chips(buf, name):
    nchip, r, cdim = buf.shape

    def body(b_ref, rcv_ref, send_sems, recv_sems, local_sem):
        x, y, c, chips = _place()
        my_chip = 2 * x + y
        mine = pltpu.make_async_copy(b_ref.at[my_chip], rcv_ref.at[my_chip], local_sem)
        mine.start()
        copies = [pltpu.make_async_remote_copy(
            src_ref=b_ref.at[2 * px + py], dst_ref=rcv_ref.at[my_chip], send_sem=send_sems.at[j],
            recv_sem=recv_sems.at[j], device_id=(px, py, c), device_id_type=MESH_ID)
            for j, (px, py) in enumerate(chips)]
        for cp in copies:
            cp.start()
        for j, (px, py) in enumerate(chips):
            pltpu.make_async_remote_copy(
                src_ref=b_ref.at[my_chip], dst_ref=rcv_ref.at[2 * px + py],
                send_sem=send_sems.at[j], recv_sem=recv_sems.at[j], device_id=(px, py, c),
                device_id_type=MESH_ID).wait_recv()
        for cp in copies:
            cp.wait_send()
        mine.wait()

    return pl.pallas_call(
        body, name=name, in_specs=[ANY], out_specs=ANY,
        out_shape=jax.ShapeDtypeStruct((nchip, r, cdim), buf.dtype),
        scratch_shapes=[pltpu.SemaphoreType.DMA((3,)), pltpu.SemaphoreType.DMA((3,)),
                        pltpu.SemaphoreType.DMA(())],
    )(buf)


def _pair_add(buf, rcv, core, name):
    _, nchip, r, cdim = buf.shape
    tr = _pick(r, (FLAT_ROWS, 8))

    def body(c_ref, a_ref, b_ref, o_ref):
        o_ref[...] = (a_ref[...] + b_ref[...]).astype(BF16)

    grid_spec = pltpu.PrefetchScalarGridSpec(
        num_scalar_prefetch=1, grid=(nchip, r // tr),
        in_specs=[pl.BlockSpec((None, None, tr, cdim), lambda k, i, c_ref: (c_ref[0], k, i, 0)),
                  pl.BlockSpec((None, tr, cdim), lambda k, i, c_ref: (k, i, 0))],
        out_specs=pl.BlockSpec((None, tr, cdim), lambda k, i, c_ref: (k, i, 0)))
    return pl.pallas_call(
        body, name=name, grid_spec=grid_spec,
        out_shape=jax.ShapeDtypeStruct((nchip, r, cdim), BF16),
        compiler_params=_params("parallel", "parallel"))(core, buf, rcv)


def _adamw(parts, w, m, v, name):
    npart, r, cdim = parts.shape
    tr = _pick(r, (FLAT_ROWS, 64, 32, 8))
    c1 = 1.0 - ADAM_B1 ** ADAM_STEP
    c2 = 1.0 - ADAM_B2 ** ADAM_STEP

    def body(p_ref, w_ref, m_ref, v_ref, g_ref, d_ref, mo_ref, vo_ref):
        g = p_ref[0].astype(F32)
        for k in range(1, npart):
            g = g + p_ref[k].astype(F32)
        mn = ADAM_B1 * m_ref[...] + (1.0 - ADAM_B1) * g
        vn = ADAM_B2 * v_ref[...] + (1.0 - ADAM_B2) * (g * g)
        g_ref[...] = g
        mo_ref[...] = mn
        vo_ref[...] = vn
        d_ref[...] = -ADAM_LR * ((mn / c1) / (jnp.sqrt(vn / c2) + ADAM_EPS) + ADAM_WD * w_ref[...])

    row = pl.BlockSpec((tr, cdim), lambda i: (i, 0))
    return pl.pallas_call(
        body, name=name, grid=(r // tr,),
        in_specs=[pl.BlockSpec((npart, tr, cdim), lambda i: (0, i, 0)), row, row, row],
        out_specs=[row] * 4, out_shape=[jax.ShapeDtypeStruct((r, cdim), F32)] * 4,
        compiler_params=_params("parallel"))(parts, w, m, v)


def _flat_pad(a, mult):
    a = a.reshape(-1)
    return jnp.pad(a, (0, _round_up(a.shape[0], mult) - a.shape[0]))


def _piece_len(shape):
    per_layer = int(math.prod(shape[1:]))
    return shape[0] * _round_up(per_layer, LANES)


def _pack(arrays, total_mult):
    flat = [jnp.pad(a.reshape(a.shape[0], -1),
                    ((0, 0), (0, _piece_len(a.shape) // a.shape[0] - int(math.prod(a.shape[1:])))))
            .reshape(-1) for a in arrays]
    return _flat_pad(jnp.concatenate(flat), total_mult).reshape(-1, LANES)


def _unpack(flat, shapes):
    flat = flat.reshape(-1)
    out, off = [], 0
    for shp in shapes:
        n = _piece_len(shp)
        per = int(math.prod(shp[1:]))
        out.append(flat[off:off + n].reshape(shp[0], -1)[:, :per].reshape(shp))
        off += n
    return out


def _full_weight(piece, name, layer):
    blk = piece[:, layer]
    if name in COL_SHARDED:
        return blk.transpose(1, 0, 2).reshape(blk.shape[1], -1)
    return blk.reshape(-1, blk.shape[2])


def _by_destination(name, grads):
    g = jnp.stack(grads)
    if name in COL_SHARDED:
        d, r, ccol = g.shape
        return g.reshape(d, r, N_DEV, ccol // N_DEV).transpose(2, 0, 1, 3)
    d, r, ccol = g.shape
    return g.reshape(d, N_DEV, r // N_DEV, ccol).transpose(1, 0, 2, 3)


def _heads(a):
    s = a.shape[0]
    return a.reshape(s, FOX_HEADS, FOX_DH).transpose(1, 0, 2)


def _unheads(a):
    return a.transpose(1, 0, 2).reshape(a.shape[1], FOX_W)


def kernel(*args):
    p = dict(zip(INPUTS, args))
    x0 = p["x"][0]
    mem = p["mem"][0]
    tgt = p["loss_target"][0]
    s, d = x0.shape
    core = lax.axis_index("c").astype(jnp.int32).reshape(1)

    mm_names = tuple(n for n in SHARDED if n != "conv_w")
    shard_shapes = {n: p[n].shape for n in SHARDED}
    gathered = _allgather(_pack([p[n].astype(BF16) for n in mm_names], 8 * LANES), "gather_weights")
    pieces = dict(zip(mm_names, _unpack_gathered(gathered, [shard_shapes[n] for n in mm_names])))
    conv_g = _allgather(_pack([p["conv_w"]], 8 * LANES), "gather_conv_w")
    conv_piece = _unpack_gathered(conv_g, [shard_shapes["conv_w"]])[0]

    def weights_of(layer):
        w = {n: _full_weight(pieces[n], n, layer) for n in mm_names}
        w_in = w.pop("w_in")
        fpad = jnp.pad(w_in[:, OFF_F:OFF_GP], ((0, 0), (0, ZR_W - ZR_F - FOX_HEADS)))
        w["w_cat"] = jnp.concatenate(
            [w_in[:, OFF_Q:OFF_F], w_in[:, OFF_GP:OFF_GF], w_in[:, OFF_GF:IN_W],
             w_in[:, 0:OFF_Q], fpad], axis=1)
        w["conv_w"] = _full_weight(conv_piece, "conv_w", layer)
        w["conv_b"] = p["conv_b"][layer].reshape(1, -1)
        w["pool_w"] = p["pool_w"][layer].astype(BF16)
        w["pool_scale"] = p["pool_scale"][layer].reshape(1, -1)
        w["b128"] = jnp.pad(p["b_forget"][layer], (0, 128 - FOX_HEADS)).reshape(1, 128)
        return w

    saved = []
    _, h1 = _norm_fwd(x0, None, None, p["mix_pre_g"][0], "norm_first")
    x_in = x0
    y_final = None
    for l in range(DEPTH):
        w = weights_of(l)
        sv = {"w": w, "x0": x_in, "h1": h1}
        zqkv = _mm(h1, w["w_cat"][:, :QKV_W], "nn", BF16, "mm_qkv")
        zr = _mm(h1, w["w_cat"][:, QKV_W:], "nn", F32, "mm_zr")
        c = _forget_fwd(zr, w["b128"], "forget_fwd")
        pm = _pool_fwd(zr, w["pool_w"], w["pool_scale"], "pool_fwd")
        qh, kh, vh = (_heads(zqkv[:, k * FOX_W:(k + 1) * FOX_W]) for k in range(3))
        c_t = c[:, :FOX_HEADS].T
        c_col, c_row = c_t[:, :, None], c_t[:, None, :]
        oh, lse = _fox_fwd(qh, kh, vh, c_col, c_row, "fox_fwd")
        o = _unheads(oh).astype(BF16)
        yp = _mm(pm, w["w_pool_br"], "nn", F32, "mm_pool_br")
        yf = _mm(o, w["w_fox_br"], "nn", F32, "mm_fox_br")
        merged = _gate_fwd(zr, yp, yf, "gate_fwd")
        f1 = _mm(merged, w["w_mix_out"], "nn", F32, "mm_mix_out")
        x1, h2 = _norm_fwd(x_in, f1, p["mix_post_g"][l], p["xa_pre_g"][l], "norm_mix_xa")
        _, mem_n = _norm_fwd(mem, None, None, p["mem_g"][l], "norm_mem")
        q2 = _mm(h2, w["w_xq"], "nn", BF16, "mm_xq")
        kv = _mm(mem_n, w["w_xkv"], "nn", BF16, "mm_xkv")
        o2 = _xattn_fwd(q2, kv, "xattn_fwd")
        f2 = _mm(o2, w["w_xo"], "nn", F32, "mm_xo")
        x2, h3 = _norm_fwd(x1, f2, p["xa_post_g"][l], p["ffn_pre_g"][l], "norm_xa_ffn")
        z3 = _mm(h3, w["w_up"], "nn", F32, "mm_up")
        a = _convglu_fwd(z3, w["conv_w"], w["conv_b"], "convglu_fwd")
        f3 = _mm(a, w["w_down"], "nn", F32, "mm_down")
        if l + 1 < DEPTH:
            x3, h1 = _norm_fwd(x2, f3, p["ffn_post_g"][l], p["mix_pre_g"][l + 1], "norm_ffn_mix")
        else:
            x3, _ = _norm_fwd(x2, f3, p["ffn_post_g"][l], None, "norm_last")
            y_final = x3
        sv.update(zr=zr, qh=qh, kh=kh, vh=vh, c_col=c_col, c_row=c_row, oh=oh, lse=lse, o=o,
                  pm=pm, yp=yp, yf=yf, merged=merged, f1=f1, x1=x1, h2=h2, mem_n=mem_n, q2=q2,
                  kv=kv, o2=o2, f2=f2, x2=x2, h3=h3, z3=z3, a=a, f3=f3)
        saved.append(sv)
        x_in = x3

    top = _norm_bwd("loss_head", y=y_final, tgt=tgt, f_prev=saved[-1]["f3"],
                    g_post=p["ffn_post_g"][DEPTH - 1])
    loss = lax.psum(top["loss"][0, 0], MESH_AXES)
    dres, df3 = top["dx"], top["df"]
    gw = {n: [None] * DEPTH for n in WEIGHTS}
    gw["ffn_post_g"][DEPTH - 1] = top["dg_post"]
    for l in reversed(range(DEPTH)):
        sv = saved[l]
        w = sv["w"]
        gw["w_down"][l] = _mm(sv["a"], df3, "tn", F32, "mm_d_w_down")
        da = _mm(df3, w["w_down"], "nt", BF16, "mm_d_a")
        dgc, duc, dcwg, dcwu, dcbg, dcbu = _convglu_bwd_pre(sv["z3"], da, w["conv_w"],
                                                            w["conv_b"], "convglu_bwd_pre")
        gw["conv_w"][l] = jnp.concatenate([dcwg, dcwu], axis=1)
        gw["conv_b"][l] = jnp.concatenate([dcbg, dcbu], axis=1)
        dz3 = jnp.concatenate(
            [_convglu_bwd_post(dgc, w["conv_w"], 0, "convglu_bwd_post_g"),
             _convglu_bwd_post(duc, w["conv_w"], D_FF // 256, "convglu_bwd_post_u")], axis=1)
        gw["w_up"][l] = _mm(sv["h3"], dz3, "tn", F32, "mm_d_w_up")
        dh3 = _mm(dz3, w["w_up"], "nt", F32, "mm_d_h3")
        nb = _norm_bwd("norm_bwd_ffn_xa", dh=dh3, x=sv["x2"], g_pre=p["ffn_pre_g"][l], dres=dres,
                       f_prev=sv["f2"], g_post=p["xa_post_g"][l])
        gw["ffn_pre_g"][l], gw["xa_post_g"][l] = nb["dg_pre"], nb["dg_post"]
        dres, df2 = nb["dx"], nb["df"]
        gw["w_xo"][l] = _mm(sv["o2"], df2, "tn", F32, "mm_d_w_xo")
        do2 = _mm(df2, w["w_xo"], "nt", BF16, "mm_d_o2")
        dq2, dk2, dv2 = _xattn_bwd(sv["q2"], sv["kv"], do2, "xattn_bwd")
        dkv = jnp.concatenate([dk2, dv2], axis=1)
        gw["w_xq"][l] = _mm(sv["h2"], dq2, "tn", F32, "mm_d_w_xq")
        dh2 = _mm(dq2, w["w_xq"], "nt", F32, "mm_d_h2")
        gw["w_xkv"][l] = _mm(sv["mem_n"], dkv, "tn", F32, "mm_d_w_xkv")
        dmem_n = _mm(dkv, w["w_xkv"], "nt", F32, "mm_d_mem")
        gw["mem_g"][l] = _rms_dg(mem, p["mem_g"][l], dmem_n, "norm_mem_bwd")
        nb = _norm_bwd("norm_bwd_xa_mix", dh=dh2, x=sv["x1"], g_pre=p["xa_pre_g"][l], dres=dres,
                       f_prev=sv["f1"], g_post=p["mix_post_g"][l])
        gw["xa_pre_g"][l], gw["mix_post_g"][l] = nb["dg_pre"], nb["dg_post"]
        dres, df1 = nb["dx"], nb["df"]
        gw["w_mix_out"][l] = _mm(sv["merged"], df1, "tn", F32, "mm_d_w_mix_out")
        dmerged = _mm(df1, w["w_mix_out"], "nt", F32, "mm_d_merged")
        dyp, dyf, dzg = _gate_bwd(dmerged, sv["zr"], sv["yp"], sv["yf"], "gate_bwd")
        gw["w_pool_br"][l] = _mm(sv["pm"], dyp, "tn", F32, "mm_d_w_pool_br")
        dpm = _mm(dyp, w["w_pool_br"], "nt", F32, "mm_d_pm")
        gw["w_fox_br"][l] = _mm(sv["o"], dyf, "tn", F32, "mm_d_w_fox_br")
        do = _mm(dyf, w["w_fox_br"], "nt", BF16, "mm_d_o")
        du, dpw, dsc = _pool_bwd(sv["zr"], dpm, w["pool_w"], w["pool_scale"], "pool_bwd")
        gw["pool_w"][l], gw["pool_scale"][l] = dpw, dsc
        dqh, dkh, dvh, dc = _fox_bwd(sv["qh"], sv["kh"], sv["vh"], sv["c_col"], sv["c_row"],
                                     sv["oh"], sv["lse"], _heads(do), "fox_bwd")
        dc_rows = jnp.pad(dc[:, 0, :].T, ((0, 0), (0, 128 - FOX_HEADS)))
        dzf, db = _forget_bwd(dc_rows, sv["zr"], w["b128"], "forget_bwd")
        gw["b_forget"][l] = db[0, :FOX_HEADS]
        dzf_pad = jnp.pad(dzf[:, :FOX_HEADS], ((0, 0), (0, ZR_W - ZR_F - FOX_HEADS)))
        dzc = jnp.concatenate([_unheads(dqh).astype(BF16), _unheads(dkh).astype(BF16),
                               _unheads(dvh).astype(BF16), dzg, du, dzf_pad.astype(BF16)], axis=1)
        dw_cat = _mm(sv["h1"], dzc, "tn", F32, "mm_d_w_in")
        gw["w_in"][l] = jnp.concatenate(
            [dw_cat[:, QKV_W + ZR_U:QKV_W + ZR_F], dw_cat[:, :QKV_W],
             dw_cat[:, QKV_W + ZR_F:QKV_W + ZR_F + FOX_HEADS],
             dw_cat[:, QKV_W + ZR_GP:QKV_W + ZR_U]], axis=1)
        dh1 = _mm(dzc, w["w_cat"], "nt", F32, "mm_d_h1")
        if l > 0:
            nb = _norm_bwd("norm_bwd_mix_ffn", dh=dh1, x=sv["x0"], g_pre=p["mix_pre_g"][l],
                           dres=dres, f_prev=saved[l - 1]["f3"], g_post=p["ffn_post_g"][l - 1])
            gw["ffn_post_g"][l - 1] = nb["dg_post"]
            df3 = nb["df"]
        else:
            nb = _norm_bwd("norm_bwd_first", dh=dh1, x=sv["x0"], g_pre=p["mix_pre_g"][l], dres=dres)
        gw["mix_pre_g"][l] = nb["dg_pre"]
        dres = nb["dx"]
    grad_x = dres[None]

    by_dest = jnp.concatenate(
        [_pad_rows(_by_destination(n, gw[n]), shard_shapes[n]) for n in SHARDED], axis=1)
    n_flat = _round_up(by_dest.shape[1], FLAT_ROWS * LANES)
    by_dest = jnp.pad(by_dest, ((0, 0), (0, n_flat - by_dest.shape[1])))
    buf = by_dest.reshape(4, 2, -1, LANES).transpose(1, 0, 2, 3)
    rcv = _swap_with_sibling(buf, "grads_to_sibling")
    pair = _pair_add(buf, rcv, core, "grads_pair_add")
    parts = _exchange_chips(pair, "grads_to_chips")
    shapes = [shard_shapes[n] for n in SHARDED]
    packed = [_pack([p[pre + n] for n in SHARDED], FLAT_ROWS * LANES) for pre in ("", "m_", "v_")]
    res_sh = [dict(zip(SHARDED, _unpack(r, shapes)))
              for r in _adamw(parts, *packed, "adamw_sharded")]

    rep_shapes = [p[n].shape for n in REPLICATED]
    part = _pack([jnp.stack(gw[n]).reshape(p[n].shape) for n in REPLICATED], 8 * LANES)
    allparts = _allgather(part, "gather_small_grads")
    packed = [_pack([p[pre + n] for n in REPLICATED], 8 * LANES) for pre in ("", "m_", "v_")]
    res_rep = [dict(zip(REPLICATED, _unpack(r, rep_shapes)))
               for r in _adamw(allparts, *packed, "adamw_replicated")]

    outs = [loss, grad_x]
    for k in range(4):
        outs += [res_sh[k][n] if n in SHARDED else res_rep[k][n] for n in WEIGHTS]
    return tuple(outs)


def _pad_rows(by_dest, shard_shape):
    per = int(math.prod(shard_shape[1:]))
    flat = by_dest.reshape(N_DEV, shard_shape[0], per)
    flat = jnp.pad(flat, ((0, 0), (0, 0), (0, _round_up(per, LANES) - per)))
    return flat.reshape(N_DEV, -1)


def _unpack_gathered(gathered, shapes):
    flat = gathered.reshape(N_DEV, -1)
    out, off = [], 0
    for shp in shapes:
        n = _piece_len(shp)
        per = int(math.prod(shp[1:]))
        out.append(flat[:, off:off + n].reshape(N_DEV, shp[0], -1)[:, :, :per]
                   .reshape((N_DEV,) + tuple(shp)))
        off += n
    return out
```

```python
import functools
import math

import jax
import jax.numpy as jnp
from jax import lax
from jax.experimental import pallas as pl
from jax.experimental.pallas import tpu as pltpu

F32 = jnp.float32
BF16 = jnp.bfloat16
MESH_AXES = ("x", "y", "c")
N_DEV = 8
MESH_ID = pl.DeviceIdType.MESH

DEPTH = 4
POOL_WINDOWS = (2, 4, 8, 16)
POOL_GROUP = 128
POOL_W = 512
POOL_HALO = 16
FOX_HEADS = 8
FOX_DH = 64
FOX_W = 512
X_HEADS = 4
X_DH = 128
X_W = 512
D_FF = 2816
RMS_EPS = 1e-6
ADAM_LR, ADAM_B1, ADAM_B2, ADAM_EPS, ADAM_WD, ADAM_STEP = 0.001, 0.9, 0.999, 1e-08, 0.01, 10

OFF_Q, OFF_F, OFF_GP, OFF_GF, IN_W = 512, 2048, 2056, 3080, 4104
ZR_GP, ZR_GF, ZR_U, ZR_F, ZR_W = 0, 1024, 2048, 2560, 3072
QKV_W = 3 * FOX_W
ZC_W = QKV_W + ZR_W

LANES = 1024
FLAT_ROWS = 128

SHARDED = ("w_in", "w_pool_br", "w_fox_br", "w_mix_out", "w_xq", "w_xkv", "w_xo", "w_up",
           "conv_w", "w_down")
COL_SHARDED = ("w_in", "w_pool_br", "w_fox_br", "w_xo", "w_up", "conv_w")
REPLICATED = ("mix_pre_g", "mix_post_g", "b_forget", "pool_w", "pool_scale", "xa_pre_g",
              "xa_post_g", "mem_g", "ffn_pre_g", "ffn_post_g", "conv_b")
WEIGHTS = ("mix_pre_g", "mix_post_g", "w_in", "b_forget", "pool_w", "pool_scale", "w_pool_br",
           "w_fox_br", "w_mix_out", "xa_pre_g", "xa_post_g", "mem_g", "w_xq", "w_xkv", "w_xo",
           "ffn_pre_g", "ffn_post_g", "w_up", "conv_w", "conv_b", "w_down")
INPUTS = (("x", "mem") + WEIGHTS + ("loss_target",) + tuple("m_" + n for n in WEIGHTS)
          + tuple("v_" + n for n in WEIGHTS))


def _pick(n, prefs):
    for p in prefs:
        if n % p == 0:
            return p
    return n


def _ktile(k):
    if k <= 2816:
        return k
    return _pick(k, (2816, 2048, 1536, 1024, 512))


def _round_up(n, m):
    return (n + m - 1) // m * m


def _params(*sem):
    return pltpu.CompilerParams(dimension_semantics=sem)


_DIMS = {"nn": (((1,), (0,)), ((), ())), "nt": (((1,), (1,)), ((), ())),
         "tn": (((0,), (0,)), ((), ()))}


def _mm(a, b, mode, out_dtype, name):
    if mode == "nn":
        (m, k), (_, n) = a.shape, b.shape
    elif mode == "nt":
        (m, k), (n, _) = a.shape, b.shape
    else:
        (k, m), (_, n) = a.shape, b.shape
    tm = _pick(m, (1024, 512, 256, 128))
    tn = _pick(n, (512, 256, 128))
    tk = _ktile(k)
    nk = k // tk

    def body(a_ref, b_ref, o_ref, acc_ref):
        kk = pl.program_id(2)

        @pl.when(kk == 0)
        def _():
            acc_ref[...] = jnp.zeros_like(acc_ref)

        acc_ref[...] += lax.dot_general(a_ref[...].astype(BF16), b_ref[...].astype(BF16),
                                        _DIMS[mode], preferred_element_type=F32)

        @pl.when(kk == nk - 1)
        def _():
            o_ref[...] = acc_ref[...].astype(out_dtype)

    if mode == "tn":
        a_spec = pl.BlockSpec((tk, tm), lambda i, j, kk: (kk, i))
    else:
        a_spec = pl.BlockSpec((tm, tk), lambda i, j, kk: (i, kk))
    if mode == "nt":
        b_spec = pl.BlockSpec((tn, tk), lambda i, j, kk: (j, kk))
    else:
        b_spec = pl.BlockSpec((tk, tn), lambda i, j, kk: (kk, j))
    return pl.pallas_call(
        body, name=name, grid=(m // tm, n // tn, nk),
        in_specs=[a_spec, b_spec],
        out_specs=pl.BlockSpec((tm, tn), lambda i, j, kk: (i, j)),
        out_shape=jax.ShapeDtypeStruct((m, n), out_dtype),
        scratch_shapes=[pltpu.VMEM((tm, tn), F32)],
        compiler_params=_params("parallel", "parallel", "arbitrary"),
    )(a, b)


def _rms(x, g):
    r = lax.rsqrt(jnp.mean(x * x, axis=-1, keepdims=True) + RMS_EPS)
    return x * r * g


def _rms_bwd(x, g, dy):
    r = lax.rsqrt(jnp.mean(x * x, axis=-1, keepdims=True) + RMS_EPS)
    xh = x * r
    t = dy * g
    dx = r * (t - xh * jnp.mean(t * xh, axis=-1, keepdims=True))
    dg = jnp.sum(dy * xh, axis=0, keepdims=True)
    return dx, dg


def _norm_fwd(x_in, f, g_post, g_pre, name):
    s, d = x_in.shape
    tm = _pick(s, (512, 256, 128))
    has_post, has_pre = f is not None, g_pre is not None
    row = pl.BlockSpec((tm, d), lambda i: (i, 0))
    vec = pl.BlockSpec((1, d), lambda i: (0, 0))

    def body(*refs):
        refs = list(refs)
        x = refs.pop(0)[...]
        if has_post:
            fv = refs.pop(0)[...]
            x = x + _rms(fv, refs.pop(0)[...])
        gpre = refs.pop(0)[...] if has_pre else None
        if has_post:
            refs.pop(0)[...] = x
        if has_pre:
            refs.pop(0)[...] = _rms(x, gpre).astype(BF16)

    ins, specs, outs, ospecs = [x_in], [row], [], []
    if has_post:
        ins += [f, g_post.reshape(1, d)]
        specs += [row, vec]
        outs.append(jax.ShapeDtypeStruct((s, d), F32))
        ospecs.append(row)
    if has_pre:
        ins.append(g_pre.reshape(1, d))
        specs.append(vec)
        outs.append(jax.ShapeDtypeStruct((s, d), BF16))
        ospecs.append(row)
    res = pl.pallas_call(body, name=name, grid=(s // tm,), in_specs=specs, out_specs=ospecs,
                         out_shape=outs, compiler_params=_params("parallel"))(*ins)
    res = list(res)
    x_out = res.pop(0) if has_post else None
    h = res.pop(0) if has_pre else None
    return x_out, h


def _norm_bwd(name, *, dh=None, x=None, g_pre=None, dres=None, y=None, tgt=None, f_prev=None,
              g_post=None):
    top = y is not None
    has_post = f_prev is not None
    ref_arr = y if top else x
    s, d = ref_arr.shape
    tm = _pick(s, (512, 256, 128))
    row = pl.BlockSpec((tm, d), lambda i: (i, 0))
    vec = pl.BlockSpec((1, d), lambda i: (0, 0))
    one = pl.BlockSpec((1, 1), lambda i: (0, 0))

    def body(*refs):
        refs = list(refs)
        i = pl.program_id(0)
        if top:
            yv, tv = refs.pop(0)[...], refs.pop(0)[...]
        else:
            dhv, xv, gv, dr = (refs.pop(0)[...].astype(F32), refs.pop(0)[...], refs.pop(0)[...],
                               refs.pop(0)[...])
        if has_post:
            fv, gp = refs.pop(0)[...], refs.pop(0)[...]
        dx_ref = refs.pop(0)
        if top:
            loss_ref = refs.pop(0)
        else:
            dgpre_ref = refs.pop(0)
        if has_post:
            df_ref, dgpost_ref = refs.pop(0), refs.pop(0)

        if top:
            err = yv - tv
            dx = err * (1.0 / d)
            part = 0.5 * jnp.sum(jnp.sum(err * err, axis=-1, keepdims=True) * (1.0 / d),
                                 axis=0, keepdims=True)
        else:
            dxn, dgpre = _rms_bwd(xv, gv, dhv)
            dx = dr + dxn
        dx_ref[...] = dx
        if has_post:
            df, dgpost = _rms_bwd(fv, gp, dx)
            df_ref[...] = df.astype(BF16)

        @pl.when(i == 0)
        def _():
            if top:
                loss_ref[...] = jnp.zeros_like(loss_ref)
            else:
                dgpre_ref[...] = jnp.zeros_like(dgpre_ref)
            if has_post:
                dgpost_ref[...] = jnp.zeros_like(dgpost_ref)

        if top:
            loss_ref[...] += part
        else:
            dgpre_ref[...] += dgpre
        if has_post:
            dgpost_ref[...] += dgpost

    if top:
        ins, specs = [y, tgt], [row, row]
    else:
        ins, specs = [dh, x, g_pre.reshape(1, d), dres], [row, row, vec, row]
    if has_post:
        ins += [f_prev, g_post.reshape(1, d)]
        specs += [row, vec]
    outs, ospecs = [jax.ShapeDtypeStruct((s, d), F32)], [row]
    if top:
        outs.append(jax.ShapeDtypeStruct((1, 1), F32))
        ospecs.append(one)
    else:
        outs.append(jax.ShapeDtypeStruct((1, d), F32))
        ospecs.append(vec)
    if has_post:
        outs += [jax.ShapeDtypeStruct((s, d), BF16), jax.ShapeDtypeStruct((1, d), F32)]
        ospecs += [row, vec]
    res = list(pl.pallas_call(body, name=name, grid=(s // tm,), in_specs=specs,
                              out_specs=ospecs, out_shape=outs,
                              compiler_params=_params("arbitrary"))(*ins))
    out = {"dx": res.pop(0)}
    out["loss" if top else "dg_pre"] = res.pop(0)
    if has_post:
        out["df"], out["dg_post"] = res.pop(0), res.pop(0)
    return out


def _rms_dg(x, g, dy, name):
    s, d = x.shape
    tm = _pick(s, (256, 128))
    row = pl.BlockSpec((tm, d), lambda i: (i, 0))
    vec = pl.BlockSpec((1, d), lambda i: (0, 0))

    def body(x_ref, g_ref, dy_ref, dg_ref):
        @pl.when(pl.program_id(0) == 0)
        def _():
            dg_ref[...] = jnp.zeros_like(dg_ref)

        dg_ref[...] += _rms_bwd(x_ref[...], g_ref[...], dy_ref[...])[1]

    return pl.pallas_call(body, name=name, grid=(s // tm,), in_specs=[row, vec, row],
                          out_specs=vec, out_shape=jax.ShapeDtypeStruct((1, d), F32),
                          compiler_params=_params("arbitrary"))(x, g.reshape(1, d), dy)


def _forget_fwd(zr, b128, name):
    s = zr.shape[0]
    tm = _pick(s, (256, 128))
    fblk = ZR_F // 128

    def body(z_ref, b_ref, c_ref, carry):
        @pl.when(pl.program_id(0) == 0)
        def _():
            carry[...] = jnp.zeros_like(carry)

        a = z_ref[...] + b_ref[...]
        acc = jnp.minimum(a, 0.0) - jnp.log1p(jnp.exp(-jnp.abs(a)))
        rows = lax.broadcasted_iota(jnp.int32, acc.shape, 0)
        k = 1
        while k < tm:
            acc = acc + jnp.where(rows >= k, pltpu.roll(acc, k, 0), 0.0)
            k *= 2
        acc = acc + carry[...]
        c_ref[...] = acc
        carry[...] = acc[tm - 1:tm, :]

    return pl.pallas_call(
        body, name=name, grid=(s // tm,),
        in_specs=[pl.BlockSpec((tm, 128), lambda i: (i, fblk)),
                  pl.BlockSpec((1, 128), lambda i: (0, 0))],
        out_specs=pl.BlockSpec((tm, 128), lambda i: (i, 0)),
        out_shape=jax.ShapeDtypeStruct((s, 128), F32),
        scratch_shapes=[pltpu.VMEM((1, 128), F32)],
        compiler_params=_params("arbitrary"))(zr, b128)


def _forget_bwd(dc, zr, b128, name):
    s = zr.shape[0]
    tm = _pick(s, (256, 128))
    nt = s // tm
    fblk = ZR_F // 128

    def body(dc_ref, z_ref, b_ref, dz_ref, db_ref, carry):
        @pl.when(pl.program_id(0) == 0)
        def _():
            carry[...] = jnp.zeros_like(carry)
            db_ref[...] = jnp.zeros_like(db_ref)

        acc = dc_ref[...]
        rows = lax.broadcasted_iota(jnp.int32, acc.shape, 0)
        k = 1
        while k < tm:
            acc = acc + jnp.where(rows < tm - k, pltpu.roll(acc, tm - k, 0), 0.0)
            k *= 2
        acc = acc + carry[...]
        carry[...] = acc[0:1, :]
        a = z_ref[...] + b_ref[...]
        dz = acc / (1.0 + jnp.exp(a))
        dz_ref[...] = dz
        db_ref[...] += jnp.sum(dz, axis=0, keepdims=True)

    return pl.pallas_call(
        body, name=name, grid=(nt,),
        in_specs=[pl.BlockSpec((tm, 128), lambda i: (nt - 1 - i, 0)),
                  pl.BlockSpec((tm, 128), lambda i: (nt - 1 - i, fblk)),
                  pl.BlockSpec((1, 128), lambda i: (0, 0))],
        out_specs=[pl.BlockSpec((tm, 128), lambda i: (nt - 1 - i, 0)),
                   pl.BlockSpec((1, 128), lambda i: (0, 0))],
        out_shape=[jax.ShapeDtypeStruct((s, 128), F32), jax.ShapeDtypeStruct((1, 128), F32)],
        scratch_shapes=[pltpu.VMEM((1, 128), F32)],
        compiler_params=_params("arbitrary"))(dc, zr, b128)


def _pooled(ext, t_abs, g, w):
    e = ext[:, g * POOL_GROUP:(g + 1) * POOL_GROUP]
    acc = e
    k = 1
    while k < w:
        acc = acc + pltpu.roll(acc, k, 0)
        k *= 2
    cnt = jnp.minimum(t_abs + 1, w).astype(F32)
    return acc[POOL_HALO:] / cnt - e[POOL_HALO:]


def _pool_specs(s, tm):
    per = tm // POOL_HALO
    ublk = ZR_U // POOL_W
    return [pl.BlockSpec((tm, POOL_W), lambda i: (i, ublk)),
            pl.BlockSpec((POOL_HALO, POOL_W), lambda i: (jnp.maximum(i * per - 1, 0), ublk))]


def _pool_fwd(zr, pw, scale, name):
    s = zr.shape[0]
    tm = _pick(s, (512, 256, 128))

    def body(u_ref, h_ref, pw_ref, sc_ref, o_ref):
        i = pl.program_id(0)
        halo = jnp.where(i > 0, h_ref[...], 0.0)
        ext = jnp.concatenate([halo, u_ref[...]], axis=0)
        t_abs = i * tm + lax.broadcasted_iota(jnp.int32, (tm, 1), 0)
        outs = []
        for g, w in enumerate(POOL_WINDOWS):
            pooled = _pooled(ext, t_abs, g, w)
            outs.append(jnp.dot(pooled.astype(BF16), pw_ref[g], preferred_element_type=F32))
        o_ref[...] = (jnp.concatenate(outs, axis=1) * sc_ref[...]).astype(BF16)

    return pl.pallas_call(
        body, name=name, grid=(s // tm,),
        in_specs=_pool_specs(s, tm) + [
            pl.BlockSpec((len(POOL_WINDOWS), POOL_GROUP, POOL_GROUP), lambda i: (0, 0, 0)),
            pl.BlockSpec((1, POOL_W), lambda i: (0, 0))],
        out_specs=pl.BlockSpec((tm, POOL_W), lambda i: (i, 0)),
        out_shape=jax.ShapeDtypeStruct((s, POOL_W), BF16),
        compiler_params=_params("parallel"))(zr, zr, pw, scale)


def _pool_bwd(zr, dpm, pw, scale, name):
    s = zr.shape[0]
    tm = _pick(s, (512, 256, 128))
    nt = s // tm
    per = tm // POOL_HALO
    n_ext = tm + POOL_HALO
    ng = len(POOL_WINDOWS)

    def body(u_ref, h_ref, d_ref, dn_ref, pw_ref, sc_ref, du_ref, dpw_ref, dsc_ref):
        i = pl.program_id(0)

        @pl.when(i == 0)
        def _():
            dpw_ref[...] = jnp.zeros_like(dpw_ref)
            dsc_ref[...] = jnp.zeros_like(dsc_ref)

        halo = jnp.where(i > 0, h_ref[...], 0.0)
        ext_u = jnp.concatenate([halo, u_ref[...]], axis=0)
        nxt = jnp.where(i < nt - 1, dn_ref[...], 0.0)
        ext_d = jnp.concatenate([d_ref[...], nxt], axis=0)
        t_abs = i * tm + lax.broadcasted_iota(jnp.int32, (tm, 1), 0)
        t_ext = i * tm + lax.broadcasted_iota(jnp.int32, (n_ext, 1), 0)
        dus, dscs = [], []
        for g, w in enumerate(POOL_WINDOWS):
            sl = slice(g * POOL_GROUP, (g + 1) * POOL_GROUP)
            pooled = _pooled(ext_u, t_abs, g, w).astype(BF16)
            mixed = jnp.dot(pooled, pw_ref[g], preferred_element_type=F32)
            d_g = ext_d[:, sl]
            dscs.append(jnp.sum(d_g[:tm] * mixed, axis=0, keepdims=True))
            dmixed = (d_g * sc_ref[:, sl]).astype(BF16)
            dpw_ref[g] += lax.dot_general(pooled, dmixed[:tm], _DIMS["tn"],
                                          preferred_element_type=F32)
            dpooled = lax.dot_general(dmixed, pw_ref[g], _DIMS["nt"], preferred_element_type=F32)
            acc = dpooled / jnp.minimum(t_ext + 1, w).astype(F32)
            k = 1
            while k < w:
                acc = acc + pltpu.roll(acc, n_ext - k, 0)
                k *= 2
            dus.append(acc[:tm] - dpooled[:tm])
        du_ref[...] = jnp.concatenate(dus, axis=1).astype(BF16)
        dsc_ref[...] += jnp.concatenate(dscs, axis=1)

    return pl.pallas_call(
        body, name=name, grid=(nt,),
        in_specs=_pool_specs(s, tm) + [
            pl.BlockSpec((tm, POOL_W), lambda i: (i, 0)),
            pl.BlockSpec((POOL_HALO, POOL_W),
                         lambda i: (jnp.minimum((i + 1) * per, s // POOL_HALO - 1), 0)),
            pl.BlockSpec((ng, POOL_GROUP, POOL_GROUP), lambda i: (0, 0, 0)),
            pl.BlockSpec((1, POOL_W), lambda i: (0, 0))],
        out_specs=[pl.BlockSpec((tm, POOL_W), lambda i: (i, 0)),
                   pl.BlockSpec((ng, POOL_GROUP, POOL_GROUP), lambda i: (0, 0, 0)),
                   pl.BlockSpec((1, POOL_W), lambda i: (0, 0))],
        out_shape=[jax.ShapeDtypeStruct((s, POOL_W), BF16),
                   jax.ShapeDtypeStruct((ng, POOL_GROUP, POOL_GROUP), F32),
                   jax.ShapeDtypeStruct((1, POOL_W), F32)],
        compiler_params=_params("arbitrary"))(zr, zr, dpm, dpm, pw, scale)


def _fox_logits(q, k, cq, ck, scale, diagonal):
    sc = lax.dot_general(q, k, _DIMS["nt"], preferred_element_type=F32) * scale + cq - ck
    if diagonal:
        rows = lax.broadcasted_iota(jnp.int32, sc.shape, 0)
        cols = lax.broadcasted_iota(jnp.int32, sc.shape, 1)
        sc = jnp.where(cols <= rows, sc, -jnp.inf)
    return sc


def _fox_fwd(q, k, v, c_col, c_row, name):
    h, s, dh = q.shape
    t = _pick(s, (512, 256, 128))
    nb = s // t
    scale = 1.0 / math.sqrt(dh)

    def body(q_ref, k_ref, v_ref, cq_ref, ck_ref, o_ref, lse_ref, m_sc, l_sc, acc_sc):
        i, j = pl.program_id(1), pl.program_id(2)

        @pl.when(j == 0)
        def _():
            m_sc[...] = jnp.full_like(m_sc, -jnp.inf)
            l_sc[...] = jnp.zeros_like(l_sc)
            acc_sc[...] = jnp.zeros_like(acc_sc)

        def step(diagonal):
            sc = _fox_logits(q_ref[...], k_ref[...], cq_ref[...], ck_ref[...], scale, diagonal)
            m_new = jnp.maximum(m_sc[...], jnp.max(sc, axis=-1, keepdims=True))
            alpha = jnp.exp(m_sc[...] - m_new)
            p = jnp.exp(sc - m_new)
            l_sc[...] = alpha * l_sc[...] + jnp.sum(p, axis=-1, keepdims=True)
            p_hi = p.astype(BF16)
            p_lo = (p - p_hi.astype(F32)).astype(BF16)
            vv = v_ref[...]
            acc_sc[...] = (alpha * acc_sc[...] + jnp.dot(p_hi, vv, preferred_element_type=F32)
                           + jnp.dot(p_lo, vv, preferred_element_type=F32))
            m_sc[...] = m_new

        pl.when(j < i)(functools.partial(step, False))
        pl.when(j == i)(functools.partial(step, True))

        @pl.when(j == nb - 1)
        def _():
            o_ref[...] = acc_sc[...] / l_sc[...]
            lse_ref[...] = m_sc[...] + jnp.log(l_sc[...])

    qspec = pl.BlockSpec((None, t, dh), lambda hh, i, j: (hh, i, 0))
    kspec = pl.BlockSpec((None, t, dh), lambda hh, i, j: (hh, jnp.minimum(j, i), 0))
    colspec = pl.BlockSpec((None, t, 1), lambda hh, i, j: (hh, i, 0))
    return pl.pallas_call(
        body, name=name, grid=(h, nb, nb),
        in_specs=[qspec, kspec, kspec, colspec,
                  pl.BlockSpec((None, 1, t), lambda hh, i, j: (hh, 0, jnp.minimum(j, i)))],
        out_specs=[qspec, colspec],
        out_shape=[jax.ShapeDtypeStruct((h, s, dh), F32), jax.ShapeDtypeStruct((h, s, 1), F32)],
        scratch_shapes=[pltpu.VMEM((t, 1), F32), pltpu.VMEM((t, 1), F32),
                        pltpu.VMEM((t, dh), F32)],
        compiler_params=_params("parallel", "parallel", "arbitrary"))(q, k, v, c_col, c_row)


def _fox_bwd(q, k, v, c_col, c_row, o, lse, do, name):
    h, s, dh = q.shape
    t = _pick(s, (512, 256, 128))
    nb = s // t
    scale = 1.0 / math.sqrt(dh)

    def body(q_ref, k_ref, v_ref, cq_ref, ck_ref, o_ref, lse_ref, do_ref,
             dq_ref, dk_ref, dv_ref, dc_ref, dk_sc, dv_sc, dc_sc):
        j, i = pl.program_id(1), pl.program_id(2)

        @pl.when((j == 0) & (i == 0))
        def _():
            dq_ref[...] = jnp.zeros_like(dq_ref)

        @pl.when(i == 0)
        def _():
            dk_sc[...] = jnp.zeros_like(dk_sc)
            dv_sc[...] = jnp.zeros_like(dv_sc)
            dc_sc[...] = jnp.zeros_like(dc_sc)

        def step(diagonal):
            qv, kv, dov = q_ref[...], k_ref[...], do_ref[...]
            sc = _fox_logits(qv, kv, cq_ref[...], ck_ref[...], scale, diagonal)
            p = jnp.exp(sc - lse_ref[...])
            delta = jnp.sum(dov.astype(F32) * o_ref[...], axis=-1, keepdims=True)
            dv_sc[...] += lax.dot_general(p.astype(BF16), dov, _DIMS["tn"],
                                          preferred_element_type=F32)
            dp = lax.dot_general(dov, v_ref[...], _DIMS["nt"], preferred_element_type=F32)
            ds = p * (dp - delta)
            dc_sc[...] += jnp.sum(ds, axis=0, keepdims=True)
            dsb = (ds * scale).astype(BF16)
            dk_sc[...] += lax.dot_general(dsb, qv, _DIMS["tn"], preferred_element_type=F32)
            rows = pl.ds(pl.multiple_of(i * t, t), t)
            dq_ref[rows, :] += jnp.dot(dsb, kv, preferred_element_type=F32)

        pl.when(i > j)(functools.partial(step, False))
        pl.when(i == j)(functools.partial(step, True))

        @pl.when(i == nb - 1)
        def _():
            dk_ref[...] = dk_sc[...]
            dv_ref[...] = dv_sc[...]
            dc_ref[...] = -dc_sc[...]

    qspec = pl.BlockSpec((None, t, dh), lambda hh, j, i: (hh, jnp.maximum(i, j), 0))
    kspec = pl.BlockSpec((None, t, dh), lambda hh, j, i: (hh, j, 0))
    qcol = pl.BlockSpec((None, t, 1), lambda hh, j, i: (hh, jnp.maximum(i, j), 0))
    krow = pl.BlockSpec((None, 1, t), lambda hh, j, i: (hh, 0, j))
    return pl.pallas_call(
        body, name=name, grid=(h, nb, nb),
        in_specs=[qspec, kspec, kspec, qcol, krow, qspec, qcol, qspec],
        out_specs=[pl.BlockSpec((None, s, dh), lambda hh, j, i: (hh, 0, 0)), kspec, kspec, krow],
        out_shape=[jax.ShapeDtypeStruct((h, s, dh), F32), jax.ShapeDtypeStruct((h, s, dh), F32),
                   jax.ShapeDtypeStruct((h, s, dh), F32), jax.ShapeDtypeStruct((h, 1, s), F32)],
        scratch_shapes=[pltpu.VMEM((t, dh), F32), pltpu.VMEM((t, dh), F32),
                        pltpu.VMEM((1, t), F32)],
        compiler_params=_params("parallel", "arbitrary", "arbitrary"))(
            q, k, v, c_col, c_row, o, lse, do)


def _gate_fwd(zr, yp, yf, name):
    s, d = yp.shape
    tm = _pick(s, (256, 128))
    row = pl.BlockSpec((tm, d), lambda i: (i, 0))

    def body(zp_ref, zf_ref, yp_ref, yf_ref, o_ref):
        o_ref[...] = (jax.nn.sigmoid(zp_ref[...]) * yp_ref[...]
                      + jax.nn.sigmoid(zf_ref[...]) * yf_ref[...]).astype(BF16)

    return pl.pallas_call(
        body, name=name, grid=(s // tm,),
        in_specs=[pl.BlockSpec((tm, d), lambda i: (i, ZR_GP // d)),
                  pl.BlockSpec((tm, d), lambda i: (i, ZR_GF // d)), row, row],
        out_specs=row, out_shape=jax.ShapeDtypeStruct((s, d), BF16),
        compiler_params=_params("parallel"))(zr, zr, yp, yf)


def _gate_bwd(dm, zr, yp, yf, name):
    s, d = yp.shape
    tm = _pick(s, (256, 128))
    row = pl.BlockSpec((tm, d), lambda i: (i, 0))

    def body(dm_ref, zp_ref, zf_ref, yp_ref, yf_ref, dyp_ref, dyf_ref, dz_ref):
        dmv = dm_ref[...]
        gp, gf = jax.nn.sigmoid(zp_ref[...]), jax.nn.sigmoid(zf_ref[...])
        dyp_ref[...] = (dmv * gp).astype(BF16)
        dyf_ref[...] = (dmv * gf).astype(BF16)
        dz_ref[:, :d] = (dmv * yp_ref[...] * gp * (1.0 - gp)).astype(BF16)
        dz_ref[:, d:] = (dmv * yf_ref[...] * gf * (1.0 - gf)).astype(BF16)

    return pl.pallas_call(
        body, name=name, grid=(s // tm,),
        in_specs=[row, pl.BlockSpec((tm, d), lambda i: (i, ZR_GP // d)),
                  pl.BlockSpec((tm, d), lambda i: (i, ZR_GF // d)), row, row],
        out_specs=[row, row, pl.BlockSpec((tm, 2 * d), lambda i: (i, 0))],
        out_shape=[jax.ShapeDtypeStruct((s, d), BF16), jax.ShapeDtypeStruct((s, d), BF16),
                   jax.ShapeDtypeStruct((s, 2 * d), BF16)],
        compiler_params=_params("parallel"))(dm, zr, zr, yp, yf)


def _xattn_probs(q, k):
    sc = lax.dot_general(q, k, _DIMS["nt"], preferred_element_type=F32) * (1.0 / math.sqrt(X_DH))
    p = jnp.exp(sc - jnp.max(sc, axis=-1, keepdims=True))
    return p / jnp.sum(p, axis=-1, keepdims=True)


def _xattn_fwd(q, kv, name):
    s = q.shape[0]
    m = kv.shape[0]
    tq = _pick(s, (512, 256, 128))

    def body(q_ref, k_ref, v_ref, o_ref):
        p = _xattn_probs(q_ref[...], k_ref[...])
        o_ref[...] = jnp.dot(p.astype(BF16), v_ref[...],
                             preferred_element_type=F32).astype(BF16)

    qspec = pl.BlockSpec((tq, X_DH), lambda i, hh: (i, hh))
    return pl.pallas_call(
        body, name=name, grid=(s // tq, X_HEADS),
        in_specs=[qspec, pl.BlockSpec((m, X_DH), lambda i, hh: (0, hh)),
                  pl.BlockSpec((m, X_DH), lambda i, hh: (0, X_HEADS + hh))],
        out_specs=qspec, out_shape=jax.ShapeDtypeStruct((s, X_W), BF16),
        compiler_params=_params("parallel", "parallel"))(q, kv, kv)


def _xattn_bwd(q, kv, do, name):
    s = q.shape[0]
    m = kv.shape[0]
    tq = _pick(s, (512, 256, 128))
    scale = 1.0 / math.sqrt(X_DH)

    def body(q_ref, k_ref, v_ref, do_ref, dq_ref, dk_ref, dv_ref):
        @pl.when(pl.program_id(1) == 0)
        def _():
            dk_ref[...] = jnp.zeros_like(dk_ref)
            dv_ref[...] = jnp.zeros_like(dv_ref)

        qv, kk, dov = q_ref[...], k_ref[...], do_ref[...]
        p = _xattn_probs(qv, kk)
        dv_ref[...] += lax.dot_general(p.astype(BF16), dov, _DIMS["tn"],
                                       preferred_element_type=F32)
        dp = lax.dot_general(dov, v_ref[...], _DIMS["nt"], preferred_element_type=F32)
        ds = p * (dp - jnp.sum(dp * p, axis=-1, keepdims=True))
        dsb = (ds * scale).astype(BF16)
        dq_ref[...] = jnp.dot(dsb, kk, preferred_element_type=F32).astype(BF16)
        dk_ref[...] += lax.dot_general(dsb, qv, _DIMS["tn"], preferred_element_type=F32)

    qspec = pl.BlockSpec((tq, X_DH), lambda hh, i: (i, hh))
    kspec = pl.BlockSpec((m, X_DH), lambda hh, i: (0, hh))
    return pl.pallas_call(
        body, name=name, grid=(X_HEADS, s // tq),
        in_specs=[qspec, kspec, pl.BlockSpec((m, X_DH), lambda hh, i: (0, X_HEADS + hh)), qspec],
        out_specs=[qspec, kspec, kspec],
        out_shape=[jax.ShapeDtypeStruct((s, X_W), BF16), jax.ShapeDtypeStruct((m, X_W), F32),
                   jax.ShapeDtypeStruct((m, X_W), F32)],
        compiler_params=_params("parallel", "arbitrary"))(q, kv, kv, do)


_GELU_C = math.sqrt(2.0 / math.pi)
CONV_HALO = 8


def _gelu(x):
    return 0.5 * x * (1.0 + jnp.tanh(_GELU_C * (x + 0.044715 * x * x * x)))


def _gelu_grad(x):
    th = jnp.tanh(_GELU_C * (x + 0.044715 * x * x * x))
    return 0.5 * (1.0 + th) + 0.5 * x * (1.0 - th * th) * _GELU_C * (1.0 + 3 * 0.044715 * x * x)


def _shift_down(z, halo, n):
    rows = lax.broadcasted_iota(jnp.int32, z.shape, 0)
    out = pltpu.roll(z, n, 0)
    for r in range(n):
        out = jnp.where(rows == r, halo[CONV_HALO - n + r:CONV_HALO - n + r + 1], out)
    return out


def _conv(z, halo, cw, cb):
    return cw[2:3] * z + cw[1:2] * _shift_down(z, halo, 1) + cw[0:1] * _shift_down(z, halo, 2) + cb


def _conv_specs(tm, tn, off):
    per = tm // CONV_HALO
    return [pl.BlockSpec((tm, tn), lambda j, i: (i, j + off)),
            pl.BlockSpec((CONV_HALO, tn), lambda j, i: (jnp.maximum(i * per - 1, 0), j + off)),
            pl.BlockSpec((3, tn), lambda j, i: (0, j + off)),
            pl.BlockSpec((1, tn), lambda j, i: (0, j + off))]


def _convglu_fwd(z, cw, cb, name):
    s = z.shape[0]
    tm = _pick(s, (512, 256, 128))
    tn = 256
    nj = D_FF // tn

    def body(zg_ref, hg_ref, cwg_ref, cbg_ref, zu_ref, hu_ref, cwu_ref, cbu_ref, a_ref):
        first = pl.program_id(1) == 0
        gc = _conv(zg_ref[...], jnp.where(first, 0.0, hg_ref[...]), cwg_ref[...], cbg_ref[...])
        uc = _conv(zu_ref[...], jnp.where(first, 0.0, hu_ref[...]), cwu_ref[...], cbu_ref[...])
        a_ref[...] = (_gelu(gc) * uc).astype(BF16)

    return pl.pallas_call(
        body, name=name, grid=(nj, s // tm),
        in_specs=_conv_specs(tm, tn, 0) + _conv_specs(tm, tn, nj),
        out_specs=pl.BlockSpec((tm, tn), lambda j, i: (i, j)),
        out_shape=jax.ShapeDtypeStruct((s, D_FF), BF16),
        compiler_params=_params("parallel", "parallel"))(z, z, cw, cb, z, z, cw, cb)


def _convglu_bwd_pre(z, da, cw, cb, name):
    s = z.shape[0]
    tm = _pick(s, (512, 256, 128))
    tn = 256
    nj = D_FF // tn

    def body(zg_ref, hg_ref, cwg_ref, cbg_ref, zu_ref, hu_ref, cwu_ref, cbu_ref, da_ref,
             dg_ref, du_ref, dcwg_ref, dcwu_ref, dcbg_ref, dcbu_ref):
        first = pl.program_id(1) == 0

        @pl.when(first)
        def _():
            for r in (dcwg_ref, dcwu_ref, dcbg_ref, dcbu_ref):
                r[...] = jnp.zeros_like(r)

        dav = da_ref[...].astype(F32)
        zg, zu = zg_ref[...], zu_ref[...]
        hg = jnp.where(first, 0.0, hg_ref[...])
        hu = jnp.where(first, 0.0, hu_ref[...])
        gc = _conv(zg, hg, cwg_ref[...], cbg_ref[...])
        uc = _conv(zu, hu, cwu_ref[...], cbu_ref[...])
        dgc = dav * uc * _gelu_grad(gc)
        duc = dav * _gelu(gc)
        dg_ref[...] = dgc
        du_ref[...] = duc
        for d, zz, hh, dcw_ref, dcb_ref in ((dgc, zg, hg, dcwg_ref, dcbg_ref),
                                            (duc, zu, hu, dcwu_ref, dcbu_ref)):
            dcw_ref[...] += jnp.concatenate(
                [jnp.sum(d * _shift_down(zz, hh, 2), axis=0, keepdims=True),
                 jnp.sum(d * _shift_down(zz, hh, 1), axis=0, keepdims=True),
                 jnp.sum(d * zz, axis=0, keepdims=True)], axis=0)
            dcb_ref[...] += jnp.sum(d, axis=0, keepdims=True)

    tile = pl.BlockSpec((tm, tn), lambda j, i: (i, j))
    wspec = pl.BlockSpec((3, tn), lambda j, i: (0, j))
    bspec = pl.BlockSpec((1, tn), lambda j, i: (0, j))
    return pl.pallas_call(
        body, name=name, grid=(nj, s // tm),
        in_specs=_conv_specs(tm, tn, 0) + _conv_specs(tm, tn, nj) + [tile],
        out_specs=[tile, tile, wspec, wspec, bspec, bspec],
        out_shape=[jax.ShapeDtypeStruct((s, D_FF), F32), jax.ShapeDtypeStruct((s, D_FF), F32),
                   jax.ShapeDtypeStruct((3, D_FF), F32), jax.ShapeDtypeStruct((3, D_FF), F32),
                   jax.ShapeDtypeStruct((1, D_FF), F32), jax.ShapeDtypeStruct((1, D_FF), F32)],
        compiler_params=_params("parallel", "arbitrary"))(z, z, cw, cb, z, z, cw, cb, da)


def _convglu_bwd_post(dzc, cw, off, name):
    s, n = dzc.shape
    tm = _pick(s, (512, 256, 128))
    tn = 256
    nt = s // tm
    per = tm // CONV_HALO

    def body(d_ref, n_ref, cw_ref, o_ref):
        d = d_ref[...]
        nxt = jnp.where(pl.program_id(1) == nt - 1, 0.0, n_ref[...])
        rows = lax.broadcasted_iota(jnp.int32, d.shape, 0)
        acc = cw_ref[2:3] * d
        for k in (1, 2):
            up = pltpu.roll(d, tm - k, 0)
            for r in range(k):
                up = jnp.where(rows == tm - k + r, nxt[r:r + 1], up)
            acc = acc + cw_ref[2 - k:3 - k] * up
        o_ref[...] = acc.astype(BF16)

    return pl.pallas_call(
        body, name=name, grid=(n // tn, nt),
        in_specs=[pl.BlockSpec((tm, tn), lambda j, i: (i, j)),
                  pl.BlockSpec((CONV_HALO, tn),
                               lambda j, i: (jnp.minimum((i + 1) * per, s // CONV_HALO - 1), j)),
                  pl.BlockSpec((3, tn), lambda j, i: (0, j + off))],
        out_specs=pl.BlockSpec((tm, tn), lambda j, i: (i, j)),
        out_shape=jax.ShapeDtypeStruct((s, n), BF16),
        compiler_params=_params("parallel", "parallel"))(dzc, dzc, cw)


ANY = pl.BlockSpec(memory_space=pl.ANY)


def _place():
    x, y, c = (lax.axis_index(a) for a in MESH_AXES)
    return x, y, c, [(1 - x, y), (x, 1 - y), (1 - x, 1 - y)]


def _allgather(src, name):
    r, cdim = src.shape

    def body(x_ref, out_ref, send_sems, recv_sems, local_sem):
        x, y, c, chips = _place()
        me, sibling = (x, y, c), (x, y, 1 - c)

        def row(px, py, pc):
            return out_ref.at[4 * px + 2 * py + pc]

        def copy(k, block, to, src_ref=None):
            return pltpu.make_async_remote_copy(
                src_ref=row(*block) if src_ref is None else src_ref, dst_ref=row(*block),
                send_sem=send_sems.at[k], recv_sem=recv_sems.at[k], device_id=to,
                device_id_type=MESH_ID)

        mine = pltpu.make_async_copy(x_ref, row(*me), local_sem)
        mine.start()
        first = [copy(0, me, sibling, src_ref=x_ref)]
        first += [copy(1 + j, me, (*chip, c), src_ref=x_ref) for j, chip in enumerate(chips)]
        for cp in first:
            cp.start()
        passed = [copy(4 + j, (*chip, c), sibling) for j, chip in enumerate(chips)]
        for j, chip in enumerate(chips):
            copy(1 + j, (*chip, c), me).wait_recv()
            passed[j].start()
        copy(0, sibling, me).wait_recv()
        for j, chip in enumerate(chips):
            copy(4 + j, (*chip, 1 - c), me).wait_recv()
        for cp in first + passed:
            cp.wait_send()
        mine.wait()

    return pl.pallas_call(
        body, name=name, in_specs=[ANY], out_specs=ANY,
        out_shape=jax.ShapeDtypeStruct((N_DEV, r, cdim), src.dtype),
        scratch_shapes=[pltpu.SemaphoreType.DMA((7,)), pltpu.SemaphoreType.DMA((7,)),
                        pltpu.SemaphoreType.DMA(())],
    )(src)


def _swap_with_sibling(buf, name):
    _, nchip, r, cdim = buf.shape

    def body(g_ref, rcv_ref, send_sems, recv_sems):
        x, y, c, _ = _place()
        copies = [pltpu.make_async_remote_copy(
            src_ref=g_ref.at[1 - c, k], dst_ref=rcv_ref.at[k], send_sem=send_sems.at[k],
            recv_sem=recv_sems.at[k], device_id=(x, y, 1 - c), device_id_type=MESH_ID)
            for k in range(nchip)]
        for cp in copies:
            cp.start()
        for cp in copies:
            cp.wait()

    return pl.pallas_call(
        body, name=name, in_specs=[ANY], out_specs=ANY,
        out_shape=jax.ShapeDtypeStruct((nchip, r, cdim), buf.dtype),
        scratch_shapes=[pltpu.SemaphoreType.DMA((nchip,)), pltpu.SemaphoreType.DMA((nchip,))],
    )(buf)


def _exchange_chips(buf, name):
    nchip, r, cdim = buf.shape

    def body(b_ref, rcv_ref, send_sems, recv_sems, local_sem):
        x, y, c, chips = _place()
        my_chip = 2 * x + y
        mine = pltpu.make_async_copy(b_ref.at[my_chip], rcv_ref.at[my_chip], local_sem)
        mine.start()
        copies = [pltpu.make_async_remote_copy(
            src_ref=b_ref.at[2 * px + py], dst_ref=rcv_ref.at[my_chip], send_sem=send_sems.at[j],
            recv_sem=recv_sems.at[j], device_id=(px, py, c), device_id_type=MESH_ID)
            for j, (px, py) in enumerate(chips)]
        for cp in copies:
            cp.start()
        for j, (px, py) in enumerate(chips):
            pltpu.make_async_remote_copy(
                src_ref=b_ref.at[my_chip], dst_ref=rcv_ref.at[2 * px + py],
                send_sem=send_sems.at[j], recv_sem=recv_sems.at[j], device_id=(px, py, c),
                device_id_type=MESH_ID).wait_recv()
        for cp in copies:
            cp.wait_send()
        mine.wait()

    return pl.pallas_call(
        body, name=name, in_specs=[ANY], out_specs=ANY,
        out_shape=jax.ShapeDtypeStruct((nchip, r, cdim), buf.dtype),
        scratch_shapes=[pltpu.SemaphoreType.DMA((3,)), pltpu.SemaphoreType.DMA((3,)),
                        pltpu.SemaphoreType.DMA(())],
    )(buf)


def _pair_add(buf, rcv, core, name):
    _, nchip, r, cdim = buf.shape
    tr = _pick(r, (FLAT_ROWS, 8))

    def body(c_ref, a_ref, b_ref, o_ref):
        o_ref[...] = (a_ref[...] + b_ref[...]).astype(BF16)

    grid_spec = pltpu.PrefetchScalarGridSpec(
        num_scalar_prefetch=1, grid=(nchip, r // tr),
        in_specs=[pl.BlockSpec((None, None, tr, cdim), lambda k, i, c_ref: (c_ref[0], k, i, 0)),
                  pl.BlockSpec((None, tr, cdim), lambda k, i, c_ref: (k, i, 0))],
        out_specs=pl.BlockSpec((None, tr, cdim), lambda k, i, c_ref: (k, i, 0)))
    return pl.pallas_call(
        body, name=name, grid_spec=grid_spec,
        out_shape=jax.ShapeDtypeStruct((nchip, r, cdim), BF16),
        compiler_params=_params("parallel", "parallel"))(core, buf, rcv)


def _adamw(parts, w, m, v, name):
    npart, r, cdim = parts.shape
    tr = _pick(r, (FLAT_ROWS, 64, 32, 8))
    c1 = 1.0 - ADAM_B1 ** ADAM_STEP
    c2 = 1.0 - ADAM_B2 ** ADAM_STEP

    def body(p_ref, w_ref, m_ref, v_ref, g_ref, d_ref, mo_ref, vo_ref):
        g = p_ref[0].astype(F32)
        for k in range(1, npart):
            g = g + p_ref[k].astype(F32)
        mn = ADAM_B1 * m_ref[...] + (1.0 - ADAM_B1) * g
        vn = ADAM_B2 * v_ref[...] + (1.0 - ADAM_B2) * (g * g)
        g_ref[...] = g
        mo_ref[...] = mn
        vo_ref[...] = vn
        d_ref[...] = -ADAM_LR * ((mn / c1) / (jnp.sqrt(vn / c2) + ADAM_EPS) + ADAM_WD * w_ref[...])

    row = pl.BlockSpec((tr, cdim), lambda i: (i, 0))
    return pl.pallas_call(
        body, name=name, grid=(r // tr,),
        in_specs=[pl.BlockSpec((npart, tr, cdim), lambda i: (0, i, 0)), row, row, row],
        out_specs=[row] * 4, out_shape=[jax.ShapeDtypeStruct((r, cdim), F32)] * 4,
        compiler_params=_params("parallel"))(parts, w, m, v)


def _piece_rows(shape):
    return shape[0] * _round_up(int(math.prod(shape[1:])), LANES) // LANES


def _to_rows(a, nlead):
    lead, depth = a.shape[:nlead], a.shape[nlead]
    per = int(math.prod(a.shape[nlead + 1:]))
    if per % LANES:
        a = jnp.pad(a.reshape(lead + (depth, per)),
                    [(0, 0)] * (nlead + 1) + [(0, _round_up(per, LANES) - per)])
    return a.reshape(lead + (-1, LANES))


def _pack(arrays, row_mult, nlead=0):
    rows = jnp.concatenate([_to_rows(a, nlead) for a in arrays], axis=nlead)
    pad = _round_up(rows.shape[nlead], row_mult) - rows.shape[nlead]
    return jnp.pad(rows, [(0, 0)] * nlead + [(0, pad), (0, 0)])


def _unpack(rows, shapes, nlead=0):
    lead = rows.shape[:nlead]
    out, r = [], 0
    for shp in shapes:
        n = _piece_rows(shp)
        piece = lax.slice_in_dim(rows, r, r + n, axis=nlead)
        per = int(math.prod(shp[1:]))
        if per % LANES:
            piece = piece.reshape(lead + (shp[0], -1))[..., :per]
        out.append(piece.reshape(lead + tuple(shp)))
        r += n
    return out


def _full_weight(piece, name, layer):
    blk = piece[:, layer]
    return jnp.concatenate([blk[dev] for dev in range(N_DEV)],
                           axis=1 if name in COL_SHARDED else 0)


def _by_destination(name, grads):
    g = jnp.stack(grads)
    if name in COL_SHARDED:
        cs = g.shape[2] // N_DEV
        return jnp.stack([g[:, :, dev * cs:(dev + 1) * cs] for dev in range(N_DEV)])
    rs = g.shape[1] // N_DEV
    return jnp.stack([g[:, dev * rs:(dev + 1) * rs] for dev in range(N_DEV)])


def _heads(a):
    s = a.shape[0]
    return a.reshape(s, FOX_HEADS, FOX_DH).transpose(1, 0, 2)


def _unheads(a):
    return a.transpose(1, 0, 2).reshape(a.shape[1], FOX_W)


def kernel(*args):
    p = dict(zip(INPUTS, args))
    x0 = p["x"][0]
    mem = p["mem"][0]
    tgt = p["loss_target"][0]
    s, d = x0.shape
    core = lax.axis_index("c").astype(jnp.int32).reshape(1)

    mm_names = tuple(n for n in SHARDED if n != "conv_w")
    shard_shapes = {n: p[n].shape for n in SHARDED}
    gathered = _allgather(_pack([p[n].astype(BF16) for n in mm_names], 8), "gather_weights")
    pieces = dict(zip(mm_names, _unpack(gathered, [shard_shapes[n] for n in mm_names], nlead=1)))
    conv_g = _allgather(_pack([p["conv_w"]], 8), "gather_conv_w")
    conv_piece = _unpack(conv_g, [shard_shapes["conv_w"]], nlead=1)[0]

    def weights_of(layer):
        w = {n: _full_weight(pieces[n], n, layer) for n in mm_names}
        w_in = w.pop("w_in")
        fpad = jnp.pad(w_in[:, OFF_F:OFF_GP], ((0, 0), (0, ZR_W - ZR_F - FOX_HEADS)))
        w["w_cat"] = jnp.concatenate(
            [w_in[:, OFF_Q:OFF_F], w_in[:, OFF_GP:OFF_GF], w_in[:, OFF_GF:IN_W],
             w_in[:, 0:OFF_Q], fpad], axis=1)
        w["conv_w"] = _full_weight(conv_piece, "conv_w", layer)
        w["conv_b"] = p["conv_b"][layer].reshape(1, -1)
        w["pool_w"] = p["pool_w"][layer].astype(BF16)
        w["pool_scale"] = p["pool_scale"][layer].reshape(1, -1)
        w["b128"] = jnp.pad(p["b_forget"][layer], (0, 128 - FOX_HEADS)).reshape(1, 128)
        return w

    saved = []
    _, h1 = _norm_fwd(x0, None, None, p["mix_pre_g"][0], "norm_first")
    x_in = x0
    y_final = None
    for l in range(DEPTH):
        w = weights_of(l)
        sv = {"w": w, "x0": x_in, "h1": h1}
        zqkv = _mm(h1, w["w_cat"][:, :QKV_W], "nn", BF16, "mm_qkv")
        zr = _mm(h1, w["w_cat"][:, QKV_W:], "nn", F32, "mm_zr")
        c = _forget_fwd(zr, w["b128"], "forget_fwd")
        pm = _pool_fwd(zr, w["pool_w"], w["pool_scale"], "pool_fwd")
        qh, kh, vh = (_heads(zqkv[:, k * FOX_W:(k + 1) * FOX_W]) for k in range(3))
        c_t = c[:, :FOX_HEADS].T
        c_col, c_row = c_t[:, :, None], c_t[:, None, :]
        oh, lse = _fox_fwd(qh, kh, vh, c_col, c_row, "fox_fwd")
        o = _unheads(oh).astype(BF16)
        yp = _mm(pm, w["w_pool_br"], "nn", F32, "mm_pool_br")
        yf = _mm(o, w["w_fox_br"], "nn", F32, "mm_fox_br")
        merged = _gate_fwd(zr, yp, yf, "gate_fwd")
        f1 = _mm(merged, w["w_mix_out"], "nn", F32, "mm_mix_out")
        x1, h2 = _norm_fwd(x_in, f1, p["mix_post_g"][l], p["xa_pre_g"][l], "norm_mix_xa")
        _, mem_n = _norm_fwd(mem, None, None, p["mem_g"][l], "norm_mem")
        q2 = _mm(h2, w["w_xq"], "nn", BF16, "mm_xq")
        kv = _mm(mem_n, w["w_xkv"], "nn", BF16, "mm_xkv")
        o2 = _xattn_fwd(q2, kv, "xattn_fwd")
        f2 = _mm(o2, w["w_xo"], "nn", F32, "mm_xo")
        x2, h3 = _norm_fwd(x1, f2, p["xa_post_g"][l], p["ffn_pre_g"][l], "norm_xa_ffn")
        z3 = _mm(h3, w["w_up"], "nn", F32, "mm_up")
        a = _convglu_fwd(z3, w["conv_w"], w["conv_b"], "convglu_fwd")
        f3 = _mm(a, w["w_down"], "nn", F32, "mm_down")
        if l + 1 < DEPTH:
            x3, h1 = _norm_fwd(x2, f3, p["ffn_post_g"][l], p["mix_pre_g"][l + 1], "norm_ffn_mix")
        else:
            x3, _ = _norm_fwd(x2, f3, p["ffn_post_g"][l], None, "norm_last")
            y_final = x3
        sv.update(zr=zr, qh=qh, kh=kh, vh=vh, c_col=c_col, c_row=c_row, oh=oh, lse=lse, o=o,
                  pm=pm, yp=yp, yf=yf, merged=merged, f1=f1, x1=x1, h2=h2, mem_n=mem_n, q2=q2,
                  kv=kv, o2=o2, f2=f2, x2=x2, h3=h3, z3=z3, a=a, f3=f3)
        saved.append(sv)
        x_in = x3

    top = _norm_bwd("loss_head", y=y_final, tgt=tgt, f_prev=saved[-1]["f3"],
                    g_post=p["ffn_post_g"][DEPTH - 1])
    loss = lax.psum(top["loss"][0, 0], MESH_AXES)
    dres, df3 = top["dx"], top["df"]
    gw = {n: [None] * DEPTH for n in WEIGHTS}
    gw["ffn_post_g"][DEPTH - 1] = top["dg_post"]
    for l in reversed(range(DEPTH)):
        sv = saved[l]
        w = sv["w"]
        gw["w_down"][l] = _mm(sv["a"], df3, "tn", F32, "mm_d_w_down")
        da = _mm(df3, w["w_down"], "nt", BF16, "mm_d_a")
        dgc, duc, dcwg, dcwu, dcbg, dcbu = _convglu_bwd_pre(sv["z3"], da, w["conv_w"],
                                                            w["conv_b"], "convglu_bwd_pre")
        gw["conv_w"][l] = jnp.concatenate([dcwg, dcwu], axis=1)
        gw["conv_b"][l] = jnp.concatenate([dcbg, dcbu], axis=1)
        dz3 = jnp.concatenate(
            [_convglu_bwd_post(dgc, w["conv_w"], 0, "convglu_bwd_post_g"),
             _convglu_bwd_post(duc, w["conv_w"], D_FF // 256, "convglu_bwd_post_u")], axis=1)
        gw["w_up"][l] = _mm(sv["h3"], dz3, "tn", F32, "mm_d_w_up")
        dh3 = _mm(dz3, w["w_up"], "nt", F32, "mm_d_h3")
        nb = _norm_bwd("norm_bwd_ffn_xa", dh=dh3, x=sv["x2"], g_pre=p["ffn_pre_g"][l], dres=dres,
                       f_prev=sv["f2"], g_post=p["xa_post_g"][l])
        gw["ffn_pre_g"][l], gw["xa_post_g"][l] = nb["dg_pre"], nb["dg_post"]
        dres, df2 = nb["dx"], nb["df"]
        gw["w_xo"][l] = _mm(sv["o2"], df2, "tn", F32, "mm_d_w_xo")
        do2 = _mm(df2, w["w_xo"], "nt", BF16, "mm_d_o2")
        dq2, dk2, dv2 = _xattn_bwd(sv["q2"], sv["kv"], do2, "xattn_bwd")
        dkv = jnp.concatenate([dk2, dv2], axis=1)
        gw["w_xq"][l] = _mm(sv["h2"], dq2, "tn", F32, "mm_d_w_xq")
        dh2 = _mm(dq2, w["w_xq"], "nt", F32, "mm_d_h2")
        gw["w_xkv"][l] = _mm(sv["mem_n"], dkv, "tn", F32, "mm_d_w_xkv")
        dmem_n = _mm(dkv, w["w_xkv"], "nt", F32, "mm_d_mem")
        gw["mem_g"][l] = _rms_dg(mem, p["mem_g"][l], dmem_n, "norm_mem_bwd")
        nb = _norm_bwd("norm_bwd_xa_mix", dh=dh2, x=sv["x1"], g_pre=p["xa_pre_g"][l], dres=dres,
                       f_prev=sv["f1"], g_post=p["mix_post_g"][l])
        gw["xa_pre_g"][l], gw["mix_post_g"][l] = nb["dg_pre"], nb["dg_post"]
        dres, df1 = nb["dx"], nb["df"]
        gw["w_mix_out"][l] = _mm(sv["merged"], df1, "tn", F32, "mm_d_w_mix_out")
        dmerged = _mm(df1, w["w_mix_out"], "nt", F32, "mm_d_merged")
        dyp, dyf, dzg = _gate_bwd(dmerged, sv["zr"], sv["yp"], sv["yf"], "gate_bwd")
        gw["w_pool_br"][l] = _mm(sv["pm"], dyp, "tn", F32, "mm_d_w_pool_br")
        dpm = _mm(dyp, w["w_pool_br"], "nt", F32, "mm_d_pm")
        gw["w_fox_br"][l] = _mm(sv["o"], dyf, "tn", F32, "mm_d_w_fox_br")
        do = _mm(dyf, w["w_fox_br"], "nt", BF16, "mm_d_o")
        du, dpw, dsc = _pool_bwd(sv["zr"], dpm, w["pool_w"], w["pool_scale"], "pool_bwd")
        gw["pool_w"][l], gw["pool_scale"][l] = dpw, dsc
        dqh, dkh, dvh, dc = _fox_bwd(sv["qh"], sv["kh"], sv["vh"], sv["c_col"], sv["c_row"],
                                     sv["oh"], sv["lse"], _heads(do), "fox_bwd")
        dc_rows = jnp.pad(dc[:, 0, :].T, ((0, 0), (0, 128 - FOX_HEADS)))
        dzf, db = _forget_bwd(dc_rows, sv["zr"], w["b128"], "forget_bwd")
        gw["b_forget"][l] = db[0, :FOX_HEADS]
        dzf_pad = jnp.pad(dzf[:, :FOX_HEADS], ((0, 0), (0, ZR_W - ZR_F - FOX_HEADS)))
        dzc = jnp.concatenate([_unheads(dqh).astype(BF16), _unheads(dkh).astype(BF16),
                               _unheads(dvh).astype(BF16), dzg, du, dzf_pad.astype(BF16)], axis=1)
        dw_cat = _mm(sv["h1"], dzc, "tn", F32, "mm_d_w_in")
        gw["w_in"][l] = jnp.concatenate(
            [dw_cat[:, QKV_W + ZR_U:QKV_W + ZR_F], dw_cat[:, :QKV_W],
             dw_cat[:, QKV_W + ZR_F:QKV_W + ZR_F + FOX_HEADS],
             dw_cat[:, QKV_W + ZR_GP:QKV_W + ZR_U]], axis=1)
        dh1 = _mm(dzc, w["w_cat"], "nt", F32, "mm_d_h1")
        if l > 0:
            nb = _norm_bwd("norm_bwd_mix_ffn", dh=dh1, x=sv["x0"], g_pre=p["mix_pre_g"][l],
                           dres=dres, f_prev=saved[l - 1]["f3"], g_post=p["ffn_post_g"][l - 1])
            gw["ffn_post_g"][l - 1] = nb["dg_post"]
            df3 = nb["df"]
        else:
            nb = _norm_bwd("norm_bwd_first", dh=dh1, x=sv["x0"], g_pre=p["mix_pre_g"][l], dres=dres)
        gw["mix_pre_g"][l] = nb["dg_pre"]
        dres = nb["dx"]
    grad_x = dres[None]

    by_dest = _pack([_by_destination(n, gw[n]) for n in SHARDED], FLAT_ROWS, nlead=1)
    buf = by_dest.reshape(4, 2, -1, LANES).transpose(1, 0, 2, 3)
    rcv = _swap_with_sibling(buf, "grads_to_sibling")
    pair = _pair_add(buf, rcv, core, "grads_pair_add")
    parts = _exchange_chips(pair, "grads_to_chips")
    shapes = [shard_shapes[n] for n in SHARDED]
    packed = [_pack([p[pre + n] for n in SHARDED], FLAT_ROWS) for pre in ("", "m_", "v_")]
    res_sh = [dict(zip(SHARDED, _unpack(r, shapes)))
              for r in _adamw(parts, *packed, "adamw_sharded")]

    rep_shapes = [p[n].shape for n in REPLICATED]
    part = _pack([jnp.stack(gw[n]).reshape(p[n].shape) for n in REPLICATED], 8)
    allparts = _allgather(part, "gather_small_grads")
    packed = [_pack([p[pre + n] for n in REPLICATED], 8) for pre in ("", "m_", "v_")]
    res_rep = [dict(zip(REPLICATED, _unpack(r, rep_shapes)))
               for r in _adamw(allparts, *packed, "adamw_replicated")]

    outs = [loss, grad_x]
    for k in range(4):
        outs += [res_sh[k][n] if n in SHARDED else res_rep[k][n] for n in WEIGHTS]
    return tuple(outs)
```

```python
import functools
import math

import jax
import jax.numpy as jnp
from jax import lax
from jax.experimental import pallas as pl
from jax.experimental.pallas import tpu as pltpu

F32 = jnp.float32
BF16 = jnp.bfloat16
MESH_AXES = ("x", "y", "c")
N_DEV = 8
MESH_ID = pl.DeviceIdType.MESH

DEPTH = 4
POOL_WINDOWS = (2, 4, 8, 16)
POOL_GROUP = 128
POOL_W = 512
POOL_HALO = 16
FOX_HEADS = 8
FOX_DH = 64
FOX_W = 512
X_HEADS = 4
X_DH = 128
X_W = 512
D_FF = 2816
RMS_EPS = 1e-6
ADAM_LR, ADAM_B1, ADAM_B2, ADAM_EPS, ADAM_WD, ADAM_STEP = 0.001, 0.9, 0.999, 1e-08, 0.01, 10

OFF_Q, OFF_F, OFF_GP, OFF_GF, IN_W = 512, 2048, 2056, 3080, 4104
ZR_GP, ZR_GF, ZR_U, ZR_F, ZR_W = 0, 1024, 2048, 2560, 3072
QKV_W = 3 * FOX_W
ZC_W = QKV_W + ZR_W

LANES = 1024
FLAT_ROWS = 128

SHARDED = ("w_in", "w_pool_br", "w_fox_br", "w_mix_out", "w_xq", "w_xkv", "w_xo", "w_up",
           "conv_w", "w_down")
COL_SHARDED = ("w_in", "w_pool_br", "w_fox_br", "w_xo", "w_up", "conv_w")
REPLICATED = ("mix_pre_g", "mix_post_g", "b_forget", "pool_w", "pool_scale", "xa_pre_g",
              "xa_post_g", "mem_g", "ffn_pre_g", "ffn_post_g", "conv_b")
WEIGHTS = ("mix_pre_g", "mix_post_g", "w_in", "b_forget", "pool_w", "pool_scale", "w_pool_br",
           "w_fox_br", "w_mix_out", "xa_pre_g", "xa_post_g", "mem_g", "w_xq", "w_xkv", "w_xo",
           "ffn_pre_g", "ffn_post_g", "w_up", "conv_w", "conv_b", "w_down")
INPUTS = (("x", "mem") + WEIGHTS + ("loss_target",) + tuple("m_" + n for n in WEIGHTS)
          + tuple("v_" + n for n in WEIGHTS))


def _pick(n, prefs):
    for p in prefs:
        if n % p == 0:
            return p
    return n


def _ktile(k):
    if k <= 2816:
        return k
    return _pick(k, (2816, 2048, 1536, 1024, 512))


def _round_up(n, m):
    return (n + m - 1) // m * m


def _params(*sem):
    return pltpu.CompilerParams(dimension_semantics=sem)


_DIMS = {"nn": (((1,), (0,)), ((), ())), "nt": (((1,), (1,)), ((), ())),
         "tn": (((0,), (0,)), ((), ()))}


def _mm(a, b, mode, out_dtype, name):
    if mode == "nn":
        (m, k), (_, n) = a.shape, b.shape
    elif mode == "nt":
        (m, k), (n, _) = a.shape, b.shape
    else:
        (k, m), (_, n) = a.shape, b.shape
    tm = _pick(m, (1024, 512, 256, 128))
    tn = _pick(n, (512, 256, 128))
    tk = _ktile(k)
    nk = k // tk

    def body(a_ref, b_ref, o_ref, acc_ref):
        kk = pl.program_id(2)

        @pl.when(kk == 0)
        def _():
            acc_ref[...] = jnp.zeros_like(acc_ref)

        acc_ref[...] += lax.dot_general(a_ref[...].astype(BF16), b_ref[...].astype(BF16),
                                        _DIMS[mode], preferred_element_type=F32)

        @pl.when(kk == nk - 1)
        def _():
            o_ref[...] = acc_ref[...].astype(out_dtype)

    if mode == "tn":
        a_spec = pl.BlockSpec((tk, tm), lambda i, j, kk: (kk, i))
    else:
        a_spec = pl.BlockSpec((tm, tk), lambda i, j, kk: (i, kk))
    if mode == "nt":
        b_spec = pl.BlockSpec((tn, tk), lambda i, j, kk: (j, kk))
    else:
        b_spec = pl.BlockSpec((tk, tn), lambda i, j, kk: (kk, j))
    return pl.pallas_call(
        body, name=name, grid=(m // tm, n // tn, nk),
        in_specs=[a_spec, b_spec],
        out_specs=pl.BlockSpec((tm, tn), lambda i, j, kk: (i, j)),
        out_shape=jax.ShapeDtypeStruct((m, n), out_dtype),
        scratch_shapes=[pltpu.VMEM((tm, tn), F32)],
        compiler_params=_params("parallel", "parallel", "arbitrary"),
    )(a, b)


def _rms(x, g):
    r = lax.rsqrt(jnp.mean(x * x, axis=-1, keepdims=True) + RMS_EPS)
    return x * r * g


def _rms_bwd(x, g, dy):
    r = lax.rsqrt(jnp.mean(x * x, axis=-1, keepdims=True) + RMS_EPS)
    xh = x * r
    t = dy * g
    dx = r * (t - xh * jnp.mean(t * xh, axis=-1, keepdims=True))
    dg = jnp.sum(dy * xh, axis=0, keepdims=True)
    return dx, dg


def _norm_fwd(x_in, f, g_post, g_pre, name):
    s, d = x_in.shape
    tm = _pick(s, (512, 256, 128))
    has_post, has_pre = f is not None, g_pre is not None
    row = pl.BlockSpec((tm, d), lambda i: (i, 0))
    vec = pl.BlockSpec((1, d), lambda i: (0, 0))

    def body(*refs):
        refs = list(refs)
        x = refs.pop(0)[...]
        if has_post:
            fv = refs.pop(0)[...]
            x = x + _rms(fv, refs.pop(0)[...])
        gpre = refs.pop(0)[...] if has_pre else None
        if has_post:
            refs.pop(0)[...] = x
        if has_pre:
            refs.pop(0)[...] = _rms(x, gpre).astype(BF16)

    ins, specs, outs, ospecs = [x_in], [row], [], []
    if has_post:
        ins += [f, g_post.reshape(1, d)]
        specs += [row, vec]
        outs.append(jax.ShapeDtypeStruct((s, d), F32))
        ospecs.append(row)
    if has_pre:
        ins.append(g_pre.reshape(1, d))
        specs.append(vec)
        outs.append(jax.ShapeDtypeStruct((s, d), BF16))
        ospecs.append(row)
    res = pl.pallas_call(body, name=name, grid=(s // tm,), in_specs=specs, out_specs=ospecs,
                         out_shape=outs, compiler_params=_params("parallel"))(*ins)
    res = list(res)
    x_out = res.pop(0) if has_post else None
    h = res.pop(0) if has_pre else None
    return x_out, h


def _norm_bwd(name, *, dh=None, x=None, g_pre=None, dres=None, y=None, tgt=None, f_prev=None,
              g_post=None):
    top = y is not None
    has_post = f_prev is not None
    ref_arr = y if top else x
    s, d = ref_arr.shape
    tm = _pick(s, (512, 256, 128))
    row = pl.BlockSpec((tm, d), lambda i: (i, 0))
    vec = pl.BlockSpec((1, d), lambda i: (0, 0))
    one = pl.BlockSpec((1, 1), lambda i: (0, 0))

    def body(*refs):
        refs = list(refs)
        i = pl.program_id(0)
        if top:
            yv, tv = refs.pop(0)[...], refs.pop(0)[...]
        else:
            dhv, xv, gv, dr = (refs.pop(0)[...].astype(F32), refs.pop(0)[...], refs.pop(0)[...],
                               refs.pop(0)[...])
        if has_post:
            fv, gp = refs.pop(0)[...], refs.pop(0)[...]
        dx_ref = refs.pop(0)
        if top:
            loss_ref = refs.pop(0)
        else:
            dgpre_ref = refs.pop(0)
        if has_post:
            df_ref, dgpost_ref = refs.pop(0), refs.pop(0)

        if top:
            err = yv - tv
            dx = err * (1.0 / d)
            part = 0.5 * jnp.sum(jnp.sum(err * err, axis=-1, keepdims=True) * (1.0 / d),
                                 axis=0, keepdims=True)
        else:
            dxn, dgpre = _rms_bwd(xv, gv, dhv)
            dx = dr + dxn
        dx_ref[...] = dx
        if has_post:
            df, dgpost = _rms_bwd(fv, gp, dx)
            df_ref[...] = df.astype(BF16)

        @pl.when(i == 0)
        def _():
            if top:
                loss_ref[...] = jnp.zeros_like(loss_ref)
            else:
                dgpre_ref[...] = jnp.zeros_like(dgpre_ref)
            if has_post:
                dgpost_ref[...] = jnp.zeros_like(dgpost_ref)

        if top:
            loss_ref[...] += part
        else:
            dgpre_ref[...] += dgpre
        if has_post:
            dgpost_ref[...] += dgpost

    if top:
        ins, specs = [y, tgt], [row, row]
    else:
        ins, specs = [dh, x, g_pre.reshape(1, d), dres], [row, row, vec, row]
    if has_post:
        ins += [f_prev, g_post.reshape(1, d)]
        specs += [row, vec]
    outs, ospecs = [jax.ShapeDtypeStruct((s, d), F32)], [row]
    if top:
        outs.append(jax.ShapeDtypeStruct((1, 1), F32))
        ospecs.append(one)
    else:
        outs.append(jax.ShapeDtypeStruct((1, d), F32))
        ospecs.append(vec)
    if has_post:
        outs += [jax.ShapeDtypeStruct((s, d), BF16), jax.ShapeDtypeStruct((1, d), F32)]
        ospecs += [row, vec]
    res = list(pl.pallas_call(body, name=name, grid=(s // tm,), in_specs=specs,
                              out_specs=ospecs, out_shape=outs,
                              compiler_params=_params("arbitrary"))(*ins))
    out = {"dx": res.pop(0)}
    out["loss" if top else "dg_pre"] = res.pop(0)
    if has_post:
        out["df"], out["dg_post"] = res.pop(0), res.pop(0)
    return out


def _rms_dg(x, g, dy, name):
    s, d = x.shape
    tm = _pick(s, (256, 128))
    row = pl.BlockSpec((tm, d), lambda i: (i, 0))
    vec = pl.BlockSpec((1, d), lambda i: (0, 0))

    def body(x_ref, g_ref, dy_ref, dg_ref):
        @pl.when(pl.program_id(0) == 0)
        def _():
            dg_ref[...] = jnp.zeros_like(dg_ref)

        dg_ref[...] += _rms_bwd(x_ref[...], g_ref[...], dy_ref[...])[1]

    return pl.pallas_call(body, name=name, grid=(s // tm,), in_specs=[row, vec, row],
                          out_specs=vec, out_shape=jax.ShapeDtypeStruct((1, d), F32),
                          compiler_params=_params("arbitrary"))(x, g.reshape(1, d), dy)


def _forget_fwd(zr, b128, name):
    s = zr.shape[0]
    tm = _pick(s, (256, 128))
    fblk = ZR_F // 128

    def body(z_ref, b_ref, c_ref, carry):
        @pl.when(pl.program_id(0) == 0)
        def _():
            carry[...] = jnp.zeros_like(carry)

        a = z_ref[...] + b_ref[...]
        acc = jnp.minimum(a, 0.0) - jnp.log1p(jnp.exp(-jnp.abs(a)))
        rows = lax.broadcasted_iota(jnp.int32, acc.shape, 0)
        k = 1
        while k < tm:
            acc = acc + jnp.where(rows >= k, pltpu.roll(acc, k, 0), 0.0)
            k *= 2
        acc = acc + carry[...]
        c_ref[...] = acc
        carry[...] = acc[tm - 1:tm, :]

    return pl.pallas_call(
        body, name=name, grid=(s // tm,),
        in_specs=[pl.BlockSpec((tm, 128), lambda i: (i, fblk)),
                  pl.BlockSpec((1, 128), lambda i: (0, 0))],
        out_specs=pl.BlockSpec((tm, 128), lambda i: (i, 0)),
        out_shape=jax.ShapeDtypeStruct((s, 128), F32),
        scratch_shapes=[pltpu.VMEM((1, 128), F32)],
        compiler_params=_params("arbitrary"))(zr, b128)


def _forget_bwd(dc, zr, b128, name):
    s = zr.shape[0]
    tm = _pick(s, (256, 128))
    nt = s // tm
    fblk = ZR_F // 128

    def body(dc_ref, z_ref, b_ref, dz_ref, db_ref, carry):
        @pl.when(pl.program_id(0) == 0)
        def _():
            carry[...] = jnp.zeros_like(carry)
            db_ref[...] = jnp.zeros_like(db_ref)

        acc = dc_ref[...]
        rows = lax.broadcasted_iota(jnp.int32, acc.shape, 0)
        k = 1
        while k < tm:
            acc = acc + jnp.where(rows < tm - k, pltpu.roll(acc, tm - k, 0), 0.0)
            k *= 2
        acc = acc + carry[...]
        carry[...] = acc[0:1, :]
        a = z_ref[...] + b_ref[...]
        dz = acc / (1.0 + jnp.exp(a))
        dz_ref[...] = dz
        db_ref[...] += jnp.sum(dz, axis=0, keepdims=True)

    return pl.pallas_call(
        body, name=name, grid=(nt,),
        in_specs=[pl.BlockSpec((tm, 128), lambda i: (nt - 1 - i, 0)),
                  pl.BlockSpec((tm, 128), lambda i: (nt - 1 - i, fblk)),
                  pl.BlockSpec((1, 128), lambda i: (0, 0))],
        out_specs=[pl.BlockSpec((tm, 128), lambda i: (nt - 1 - i, 0)),
                   pl.BlockSpec((1, 128), lambda i: (0, 0))],
        out_shape=[jax.ShapeDtypeStruct((s, 128), F32), jax.ShapeDtypeStruct((1, 128), F32)],
        scratch_shapes=[pltpu.VMEM((1, 128), F32)],
        compiler_params=_params("arbitrary"))(dc, zr, b128)


def _pooled(ext, t_abs, g, w):
    e = ext[:, g * POOL_GROUP:(g + 1) * POOL_GROUP]
    acc = e
    k = 1
    while k < w:
        acc = acc + pltpu.roll(acc, k, 0)
        k *= 2
    cnt = jnp.minimum(t_abs + 1, w).astype(F32)
    return acc[POOL_HALO:] / cnt - e[POOL_HALO:]


def _pool_specs(s, tm):
    per = tm // POOL_HALO
    ublk = ZR_U // POOL_W
    return [pl.BlockSpec((tm, POOL_W), lambda i: (i, ublk)),
            pl.BlockSpec((POOL_HALO, POOL_W), lambda i: (jnp.maximum(i * per - 1, 0), ublk))]


def _pool_fwd(zr, pw, scale, name):
    s = zr.shape[0]
    tm = _pick(s, (512, 256, 128))

    def body(u_ref, h_ref, pw_ref, sc_ref, o_ref):
        i = pl.program_id(0)
        halo = jnp.where(i > 0, h_ref[...], 0.0)
        ext = jnp.concatenate([halo, u_ref[...]], axis=0)
        t_abs = i * tm + lax.broadcasted_iota(jnp.int32, (tm, 1), 0)
        outs = []
        for g, w in enumerate(POOL_WINDOWS):
            pooled = _pooled(ext, t_abs, g, w)
            outs.append(jnp.dot(pooled.astype(BF16), pw_ref[g], preferred_element_type=F32))
        o_ref[...] = (jnp.concatenate(outs, axis=1) * sc_ref[...]).astype(BF16)

    return pl.pallas_call(
        body, name=name, grid=(s // tm,),
        in_specs=_pool_specs(s, tm) + [
            pl.BlockSpec((len(POOL_WINDOWS), POOL_GROUP, POOL_GROUP), lambda i: (0, 0, 0)),
            pl.BlockSpec((1, POOL_W), lambda i: (0, 0))],
        out_specs=pl.BlockSpec((tm, POOL_W), lambda i: (i, 0)),
        out_shape=jax.ShapeDtypeStruct((s, POOL_W), BF16),
        compiler_params=_params("parallel"))(zr, zr, pw, scale)


def _pool_bwd(zr, dpm, pw, scale, name):
    s = zr.shape[0]
    tm = _pick(s, (512, 256, 128))
    nt = s // tm
    per = tm // POOL_HALO
    n_ext = tm + POOL_HALO
    ng = len(POOL_WINDOWS)

    def body(u_ref, h_ref, d_ref, dn_ref, pw_ref, sc_ref, du_ref, dpw_ref, dsc_ref):
        i = pl.program_id(0)

        @pl.when(i == 0)
        def _():
            dpw_ref[...] = jnp.zeros_like(dpw_ref)
            dsc_ref[...] = jnp.zeros_like(dsc_ref)

        halo = jnp.where(i > 0, h_ref[...], 0.0)
        ext_u = jnp.concatenate([halo, u_ref[...]], axis=0)
        nxt = jnp.where(i < nt - 1, dn_ref[...], 0.0)
        ext_d = jnp.concatenate([d_ref[...], nxt], axis=0)
        t_abs = i * tm + lax.broadcasted_iota(jnp.int32, (tm, 1), 0)
        t_ext = i * tm + lax.broadcasted_iota(jnp.int32, (n_ext, 1), 0)
        dus, dscs = [], []
        for g, w in enumerate(POOL_WINDOWS):
            sl = slice(g * POOL_GROUP, (g + 1) * POOL_GROUP)
            pooled = _pooled(ext_u, t_abs, g, w).astype(BF16)
            mixed = jnp.dot(pooled, pw_ref[g], preferred_element_type=F32)
            d_g = ext_d[:, sl]
            dscs.append(jnp.sum(d_g[:tm] * mixed, axis=0, keepdims=True))
            dmixed = (d_g * sc_ref[:, sl]).astype(BF16)
            dpw_ref[g] += lax.dot_general(pooled, dmixed[:tm], _DIMS["tn"],
                                          preferred_element_type=F32)
            dpooled = lax.dot_general(dmixed, pw_ref[g], _DIMS["nt"], preferred_element_type=F32)
            acc = dpooled / jnp.minimum(t_ext + 1, w).astype(F32)
            k = 1
            while k < w:
                acc = acc + pltpu.roll(acc, n_ext - k, 0)
                k *= 2
            dus.append(acc[:tm] - dpooled[:tm])
        du_ref[...] = jnp.concatenate(dus, axis=1).astype(BF16)
        dsc_ref[...] += jnp.concatenate(dscs, axis=1)

    return pl.pallas_call(
        body, name=name, grid=(nt,),
        in_specs=_pool_specs(s, tm) + [
            pl.BlockSpec((tm, POOL_W), lambda i: (i, 0)),
            pl.BlockSpec((POOL_HALO, POOL_W),
                         lambda i: (jnp.minimum((i + 1) * per, s // POOL_HALO - 1), 0)),
            pl.BlockSpec((ng, POOL_GROUP, POOL_GROUP), lambda i: (0, 0, 0)),
            pl.BlockSpec((1, POOL_W), lambda i: (0, 0))],
        out_specs=[pl.BlockSpec((tm, POOL_W), lambda i: (i, 0)),
                   pl.BlockSpec((ng, POOL_GROUP, POOL_GROUP), lambda i: (0, 0, 0)),
                   pl.BlockSpec((1, POOL_W), lambda i: (0, 0))],
        out_shape=[jax.ShapeDtypeStruct((s, POOL_W), BF16),
                   jax.ShapeDtypeStruct((ng, POOL_GROUP, POOL_GROUP), F32),
                   jax.ShapeDtypeStruct((1, POOL_W), F32)],
        compiler_params=_params("arbitrary"))(zr, zr, dpm, dpm, pw, scale)


def _fox_logits(q, k, cq, ck, scale, diagonal):
    sc = lax.dot_general(q, k, _DIMS["nt"], preferred_element_type=F32) * scale + cq - ck
    if diagonal:
        rows = lax.broadcasted_iota(jnp.int32, sc.shape, 0)
        cols = lax.broadcasted_iota(jnp.int32, sc.shape, 1)
        sc = jnp.where(cols <= rows, sc, -jnp.inf)
    return sc


def _fox_fwd(q, k, v, c_col, c_row, name):
    h, s, dh = q.shape
    t = _pick(s, (512, 256, 128))
    nb = s // t
    scale = 1.0 / math.sqrt(dh)

    pairs = [(i, j) for i in range(nb) for j in range(i + 1)]
    qi_tab = jnp.asarray([pr[0] for pr in pairs], jnp.int32)
    kj_tab = jnp.asarray([pr[1] for pr in pairs], jnp.int32)

    def body(qi_ref, kj_ref, q_ref, k_ref, v_ref, cq_ref, ck_ref, o_ref, lse_ref, m_sc, l_sc,
             acc_sc):
        i, j = qi_ref[pl.program_id(1)], kj_ref[pl.program_id(1)]

        @pl.when(j == 0)
        def _():
            m_sc[...] = jnp.full_like(m_sc, -jnp.inf)
            l_sc[...] = jnp.zeros_like(l_sc)
            acc_sc[...] = jnp.zeros_like(acc_sc)

        def step(diagonal):
            sc = _fox_logits(q_ref[...], k_ref[...], cq_ref[...], ck_ref[...], scale, diagonal)
            m_new = jnp.maximum(m_sc[...], jnp.max(sc, axis=-1, keepdims=True))
            alpha = jnp.exp(m_sc[...] - m_new)
            p = jnp.exp(sc - m_new)
            l_sc[...] = alpha * l_sc[...] + jnp.sum(p, axis=-1, keepdims=True)
            p_hi = p.astype(BF16)
            p_lo = (p - p_hi.astype(F32)).astype(BF16)
            vv = v_ref[...]
            acc_sc[...] = (alpha * acc_sc[...] + jnp.dot(p_hi, vv, preferred_element_type=F32)
                           + jnp.dot(p_lo, vv, preferred_element_type=F32))
            m_sc[...] = m_new

        pl.when(j < i)(functools.partial(step, False))
        pl.when(j == i)(functools.partial(step, True))

        @pl.when(j == i)
        def _():
            o_ref[...] = acc_sc[...] / l_sc[...]
            lse_ref[...] = m_sc[...] + jnp.log(l_sc[...])

    qspec = pl.BlockSpec((None, t, dh), lambda hh, n, qi, kj: (hh, qi[n], 0))
    kspec = pl.BlockSpec((None, t, dh), lambda hh, n, qi, kj: (hh, kj[n], 0))
    colspec = pl.BlockSpec((None, t, 1), lambda hh, n, qi, kj: (hh, qi[n], 0))
    grid_spec = pltpu.PrefetchScalarGridSpec(
        num_scalar_prefetch=2, grid=(h, len(pairs)),
        in_specs=[qspec, kspec, kspec, colspec,
                  pl.BlockSpec((None, 1, t), lambda hh, n, qi, kj: (hh, 0, kj[n]))],
        out_specs=[qspec, colspec],
        scratch_shapes=[pltpu.VMEM((t, 1), F32), pltpu.VMEM((t, 1), F32),
                        pltpu.VMEM((t, dh), F32)])
    return pl.pallas_call(
        body, name=name, grid_spec=grid_spec,
        out_shape=[jax.ShapeDtypeStruct((h, s, dh), F32), jax.ShapeDtypeStruct((h, s, 1), F32)],
        compiler_params=_params("parallel", "arbitrary"))(qi_tab, kj_tab, q, k, v, c_col, c_row)


def _fox_bwd(q, k, v, c_col, c_row, o, lse, do, name):
    h, s, dh = q.shape
    t = _pick(s, (512, 256, 128))
    nb = s // t
    scale = 1.0 / math.sqrt(dh)

    pairs = [(i, j) for j in range(nb) for i in range(j, nb)]
    qi_tab = jnp.asarray([pr[0] for pr in pairs], jnp.int32)
    kj_tab = jnp.asarray([pr[1] for pr in pairs], jnp.int32)

    def body(qi_ref, kj_ref, q_ref, k_ref, v_ref, cq_ref, ck_ref, o_ref, lse_ref, do_ref,
             dq_ref, dk_ref, dv_ref, dc_ref, dk_sc, dv_sc, dc_sc):
        i, j = qi_ref[pl.program_id(1)], kj_ref[pl.program_id(1)]

        @pl.when(pl.program_id(1) == 0)
        def _():
            dq_ref[...] = jnp.zeros_like(dq_ref)

        @pl.when(i == j)
        def _():
            dk_sc[...] = jnp.zeros_like(dk_sc)
            dv_sc[...] = jnp.zeros_like(dv_sc)
            dc_sc[...] = jnp.zeros_like(dc_sc)

        def step(diagonal):
            qv, kv, dov = q_ref[...], k_ref[...], do_ref[...]
            sc = _fox_logits(qv, kv, cq_ref[...], ck_ref[...], scale, diagonal)
            p = jnp.exp(sc - lse_ref[...])
            delta = jnp.sum(dov.astype(F32) * o_ref[...], axis=-1, keepdims=True)
            dv_sc[...] += lax.dot_general(p.astype(BF16), dov, _DIMS["tn"],
                                          preferred_element_type=F32)
            dp = lax.dot_general(dov, v_ref[...], _DIMS["nt"], preferred_element_type=F32)
            ds = p * (dp - delta)
            dc_sc[...] += jnp.sum(ds, axis=0, keepdims=True)
            dsb = (ds * scale).astype(BF16)
            dk_sc[...] += lax.dot_general(dsb, qv, _DIMS["tn"], preferred_element_type=F32)
            rows = pl.ds(pl.multiple_of(i * t, t), t)
            dq_ref[rows, :] += jnp.dot(dsb, kv, preferred_element_type=F32)

        pl.when(i > j)(functools.partial(step, False))
        pl.when(i == j)(functools.partial(step, True))

        @pl.when(i == nb - 1)
        def _():
            dk_ref[...] = dk_sc[...]
            dv_ref[...] = dv_sc[...]
            dc_ref[...] = -dc_sc[...]

    qspec = pl.BlockSpec((None, t, dh), lambda hh, n, qi, kj: (hh, qi[n], 0))
    kspec = pl.BlockSpec((None, t, dh), lambda hh, n, qi, kj: (hh, kj[n], 0))
    qcol = pl.BlockSpec((None, t, 1), lambda hh, n, qi, kj: (hh, qi[n], 0))
    krow = pl.BlockSpec((None, 1, t), lambda hh, n, qi, kj: (hh, 0, kj[n]))
    grid_spec = pltpu.PrefetchScalarGridSpec(
        num_scalar_prefetch=2, grid=(h, len(pairs)),
        in_specs=[qspec, kspec, kspec, qcol, krow, qspec, qcol, qspec],
        out_specs=[pl.BlockSpec((None, s, dh), lambda hh, n, qi, kj: (hh, 0, 0)), kspec, kspec,
                   krow],
        scratch_shapes=[pltpu.VMEM((t, dh), F32), pltpu.VMEM((t, dh), F32),
                        pltpu.VMEM((1, t), F32)])
    return pl.pallas_call(
        body, name=name, grid_spec=grid_spec,
        out_shape=[jax.ShapeDtypeStruct((h, s, dh), F32), jax.ShapeDtypeStruct((h, s, dh), F32),
                   jax.ShapeDtypeStruct((h, s, dh), F32), jax.ShapeDtypeStruct((h, 1, s), F32)],
        compiler_params=_params("parallel", "arbitrary"))(
            qi_tab, kj_tab, q, k, v, c_col, c_row, o, lse, do)


def _gate_fwd(zr, yp, yf, name):
    s, d = yp.shape
    tm = _pick(s, (256, 128))
    row = pl.BlockSpec((tm, d), lambda i: (i, 0))

    def body(zp_ref, zf_ref, yp_ref, yf_ref, o_ref):
        o_ref[...] = (jax.nn.sigmoid(zp_ref[...]) * yp_ref[...]
                      + jax.nn.sigmoid(zf_ref[...]) * yf_ref[...]).astype(BF16)

    return pl.pallas_call(
        body, name=name, grid=(s // tm,),
        in_specs=[pl.BlockSpec((tm, d), lambda i: (i, ZR_GP // d)),
                  pl.BlockSpec((tm, d), lambda i: (i, ZR_GF // d)), row, row],
        out_specs=row, out_shape=jax.ShapeDtypeStruct((s, d), BF16),
        compiler_params=_params("parallel"))(zr, zr, yp, yf)


def _gate_bwd(dm, zr, yp, yf, name):
    s, d = yp.shape
    tm = _pick(s, (256, 128))
    row = pl.BlockSpec((tm, d), lambda i: (i, 0))

    def body(dm_ref, zp_ref, zf_ref, yp_ref, yf_ref, dyp_ref, dyf_ref, dz_ref):
        dmv = dm_ref[...]
        gp, gf = jax.nn.sigmoid(zp_ref[...]), jax.nn.sigmoid(zf_ref[...])
        dyp_ref[...] = (dmv * gp).astype(BF16)
        dyf_ref[...] = (dmv * gf).astype(BF16)
        dz_ref[:, :d] = (dmv * yp_ref[...] * gp * (1.0 - gp)).astype(BF16)
        dz_ref[:, d:] = (dmv * yf_ref[...] * gf * (1.0 - gf)).astype(BF16)

    return pl.pallas_call(
        body, name=name, grid=(s // tm,),
        in_specs=[row, pl.BlockSpec((tm, d), lambda i: (i, ZR_GP // d)),
                  pl.BlockSpec((tm, d), lambda i: (i, ZR_GF // d)), row, row],
        out_specs=[row, row, pl.BlockSpec((tm, 2 * d), lambda i: (i, 0))],
        out_shape=[jax.ShapeDtypeStruct((s, d), BF16), jax.ShapeDtypeStruct((s, d), BF16),
                   jax.ShapeDtypeStruct((s, 2 * d), BF16)],
        compiler_params=_params("parallel"))(dm, zr, zr, yp, yf)


def _xattn_probs(q, k):
    sc = lax.dot_general(q, k, _DIMS["nt"], preferred_element_type=F32) * (1.0 / math.sqrt(X_DH))
    p = jnp.exp(sc - jnp.max(sc, axis=-1, keepdims=True))
    return p / jnp.sum(p, axis=-1, keepdims=True)


def _xattn_fwd(q, kv, name):
    s = q.shape[0]
    m = kv.shape[0]
    tq = _pick(s, (512, 256, 128))

    def body(q_ref, k_ref, v_ref, o_ref):
        p = _xattn_probs(q_ref[...], k_ref[...])
        o_ref[...] = jnp.dot(p.astype(BF16), v_ref[...],
                             preferred_element_type=F32).astype(BF16)

    qspec = pl.BlockSpec((tq, X_DH), lambda i, hh: (i, hh))
    return pl.pallas_call(
        body, name=name, grid=(s // tq, X_HEADS),
        in_specs=[qspec, pl.BlockSpec((m, X_DH), lambda i, hh: (0, hh)),
                  pl.BlockSpec((m, X_DH), lambda i, hh: (0, X_HEADS + hh))],
        out_specs=qspec, out_shape=jax.ShapeDtypeStruct((s, X_W), BF16),
        compiler_params=_params("parallel", "parallel"))(q, kv, kv)


def _xattn_bwd(q, kv, do, name):
    s = q.shape[0]
    m = kv.shape[0]
    tq = _pick(s, (512, 256, 128))
    scale = 1.0 / math.sqrt(X_DH)

    def body(q_ref, k_ref, v_ref, do_ref, dq_ref, dk_ref, dv_ref):
        @pl.when(pl.program_id(1) == 0)
        def _():
            dk_ref[...] = jnp.zeros_like(dk_ref)
            dv_ref[...] = jnp.zeros_like(dv_ref)

        qv, kk, dov = q_ref[...], k_ref[...], do_ref[...]
        p = _xattn_probs(qv, kk)
        dv_ref[...] += lax.dot_general(p.astype(BF16), dov, _DIMS["tn"],
                                       preferred_element_type=F32)
        dp = lax.dot_general(dov, v_ref[...], _DIMS["nt"], preferred_element_type=F32)
        ds = p * (dp - jnp.sum(dp * p, axis=-1, keepdims=True))
        dsb = (ds * scale).astype(BF16)
        dq_ref[...] = jnp.dot(dsb, kk, preferred_element_type=F32).astype(BF16)
        dk_ref[...] += lax.dot_general(dsb, qv, _DIMS["tn"], preferred_element_type=F32)

    qspec = pl.BlockSpec((tq, X_DH), lambda hh, i: (i, hh))
    kspec = pl.BlockSpec((m, X_DH), lambda hh, i: (0, hh))
    return pl.pallas_call(
        body, name=name, grid=(X_HEADS, s // tq),
        in_specs=[qspec, kspec, pl.BlockSpec((m, X_DH), lambda hh, i: (0, X_HEADS + hh)), qspec],
        out_specs=[qspec, kspec, kspec],
        out_shape=[jax.ShapeDtypeStruct((s, X_W), BF16), jax.ShapeDtypeStruct((m, X_W), F32),
                   jax.ShapeDtypeStruct((m, X_W), F32)],
        compiler_params=_params("parallel", "arbitrary"))(q, kv, kv, do)


_GELU_C = math.sqrt(2.0 / math.pi)
CONV_HALO = 8


def _gelu(x):
    return 0.5 * x * (1.0 + jnp.tanh(_GELU_C * (x + 0.044715 * x * x * x)))


def _gelu_grad(x):
    th = jnp.tanh(_GELU_C * (x + 0.044715 * x * x * x))
    return 0.5 * (1.0 + th) + 0.5 * x * (1.0 - th * th) * _GELU_C * (1.0 + 3 * 0.044715 * x * x)


def _shift_down(z, halo, n):
    rows = lax.broadcasted_iota(jnp.int32, z.shape, 0)
    out = pltpu.roll(z, n, 0)
    for r in range(n):
        out = jnp.where(rows == r, halo[CONV_HALO - n + r:CONV_HALO - n + r + 1], out)
    return out


def _conv(z, halo, cw, cb):
    return cw[2:3] * z + cw[1:2] * _shift_down(z, halo, 1) + cw[0:1] * _shift_down(z, halo, 2) + cb


def _conv_specs(tm, tn, off):
    per = tm // CONV_HALO
    return [pl.BlockSpec((tm, tn), lambda j, i: (i, j + off)),
            pl.BlockSpec((CONV_HALO, tn), lambda j, i: (jnp.maximum(i * per - 1, 0), j + off)),
            pl.BlockSpec((3, tn), lambda j, i: (0, j + off)),
            pl.BlockSpec((1, tn), lambda j, i: (0, j + off))]


def _convglu_fwd(z, cw, cb, name):
    s = z.shape[0]
    tm = _pick(s, (512, 256, 128))
    tn = 256
    nj = D_FF // tn

    def body(zg_ref, hg_ref, cwg_ref, cbg_ref, zu_ref, hu_ref, cwu_ref, cbu_ref, a_ref):
        first = pl.program_id(1) == 0
        gc = _conv(zg_ref[...], jnp.where(first, 0.0, hg_ref[...]), cwg_ref[...], cbg_ref[...])
        uc = _conv(zu_ref[...], jnp.where(first, 0.0, hu_ref[...]), cwu_ref[...], cbu_ref[...])
        a_ref[...] = (_gelu(gc) * uc).astype(BF16)

    return pl.pallas_call(
        body, name=name, grid=(nj, s // tm),
        in_specs=_conv_specs(tm, tn, 0) + _conv_specs(tm, tn, nj),
        out_specs=pl.BlockSpec((tm, tn), lambda j, i: (i, j)),
        out_shape=jax.ShapeDtypeStruct((s, D_FF), BF16),
        compiler_params=_params("parallel", "parallel"))(z, z, cw, cb, z, z, cw, cb)


def _convglu_bwd_pre(z, da, cw, cb, name):
    s = z.shape[0]
    tm = _pick(s, (512, 256, 128))
    tn = 256
    nj = D_FF // tn

    def body(zg_ref, hg_ref, cwg_ref, cbg_ref, zu_ref, hu_ref, cwu_ref, cbu_ref, da_ref,
             dg_ref, du_ref, dcwg_ref, dcwu_ref, dcbg_ref, dcbu_ref):
        first = pl.program_id(1) == 0

        @pl.when(first)
        def _():
            for r in (dcwg_ref, dcwu_ref, dcbg_ref, dcbu_ref):
                r[...] = jnp.zeros_like(r)

        dav = da_ref[...].astype(F32)
        zg, zu = zg_ref[...], zu_ref[...]
        hg = jnp.where(first, 0.0, hg_ref[...])
        hu = jnp.where(first, 0.0, hu_ref[...])
        gc = _conv(zg, hg, cwg_ref[...], cbg_ref[...])
        uc = _conv(zu, hu, cwu_ref[...], cbu_ref[...])
        dgc = dav * uc * _gelu_grad(gc)
        duc = dav * _gelu(gc)
        dg_ref[...] = dgc
        du_ref[...] = duc
        for d, zz, hh, dcw_ref, dcb_ref in ((dgc, zg, hg, dcwg_ref, dcbg_ref),
                                            (duc, zu, hu, dcwu_ref, dcbu_ref)):
            dcw_ref[...] += jnp.concatenate(
                [jnp.sum(d * _shift_down(zz, hh, 2), axis=0, keepdims=True),
                 jnp.sum(d * _shift_down(zz, hh, 1), axis=0, keepdims=True),
                 jnp.sum(d * zz, axis=0, keepdims=True)], axis=0)
            dcb_ref[...] += jnp.sum(d, axis=0, keepdims=True)

    tile = pl.BlockSpec((tm, tn), lambda j, i: (i, j))
    wspec = pl.BlockSpec((3, tn), lambda j, i: (0, j))
    bspec = pl.BlockSpec((1, tn), lambda j, i: (0, j))
    return pl.pallas_call(
        body, name=name, grid=(nj, s // tm),
        in_specs=_conv_specs(tm, tn, 0) + _conv_specs(tm, tn, nj) + [tile],
        out_specs=[tile, tile, wspec, wspec, bspec, bspec],
        out_shape=[jax.ShapeDtypeStruct((s, D_FF), F32), jax.ShapeDtypeStruct((s, D_FF), F32),
                   jax.ShapeDtypeStruct((3, D_FF), F32), jax.ShapeDtypeStruct((3, D_FF), F32),
                   jax.ShapeDtypeStruct((1, D_FF), F32), jax.ShapeDtypeStruct((1, D_FF), F32)],
        compiler_params=_params("parallel", "arbitrary"))(z, z, cw, cb, z, z, cw, cb, da)


def _convglu_bwd_post(dzc, cw, off, name):
    s, n = dzc.shape
    tm = _pick(s, (512, 256, 128))
    tn = 256
    nt = s // tm
    per = tm // CONV_HALO

    def body(d_ref, n_ref, cw_ref, o_ref):
        d = d_ref[...]
        nxt = jnp.where(pl.program_id(1) == nt - 1, 0.0, n_ref[...])
        rows = lax.broadcasted_iota(jnp.int32, d.shape, 0)
        acc = cw_ref[2:3] * d
        for k in (1, 2):
            up = pltpu.roll(d, tm - k, 0)
            for r in range(k):
                up = jnp.where(rows == tm - k + r, nxt[r:r + 1], up)
            acc = acc + cw_ref[2 - k:3 - k] * up
        o_ref[...] = acc.astype(BF16)

    return pl.pallas_call(
        body, name=name, grid=(n // tn, nt),
        in_specs=[pl.BlockSpec((tm, tn), lambda j, i: (i, j)),
                  pl.BlockSpec((CONV_HALO, tn),
                               lambda j, i: (jnp.minimum((i + 1) * per, s // CONV_HALO - 1), j)),
                  pl.BlockSpec((3, tn), lambda j, i: (0, j + off))],
        out_specs=pl.BlockSpec((tm, tn), lambda j, i: (i, j)),
        out_shape=jax.ShapeDtypeStruct((s, n), BF16),
        compiler_params=_params("parallel", "parallel"))(dzc, dzc, cw)


ANY = pl.BlockSpec(memory_space=pl.ANY)


def _place():
    x, y, c = (lax.axis_index(a) for a in MESH_AXES)
    return x, y, c, [(1 - x, y), (x, 1 - y), (1 - x, 1 - y)]


def _allgather(src, name):
    r, cdim = src.shape

    def body(x_ref, out_ref, send_sems, recv_sems, local_sem):
        x, y, c, chips = _place()
        me, sibling = (x, y, c), (x, y, 1 - c)

        def row(px, py, pc):
            return out_ref.at[4 * px + 2 * py + pc]

        def copy(k, block, to, src_ref=None):
            return pltpu.make_async_remote_copy(
                src_ref=row(*block) if src_ref is None else src_ref, dst_ref=row(*block),
                send_sem=send_sems.at[k], recv_sem=recv_sems.at[k], device_id=to,
                device_id_type=MESH_ID)

        mine = pltpu.make_async_copy(x_ref, row(*me), local_sem)
        mine.start()
        first = [copy(0, me, sibling, src_ref=x_ref)]
        first += [copy(1 + j, me, (*chip, c), src_ref=x_ref) for j, chip in enumerate(chips)]
        for cp in first:
            cp.start()
        passed = [copy(4 + j, (*chip, c), sibling) for j, chip in enumerate(chips)]
        for j, chip in enumerate(chips):
            copy(1 + j, (*chip, c), me).wait_recv()
            passed[j].start()
        copy(0, sibling, me).wait_recv()
        for j, chip in enumerate(chips):
            copy(4 + j, (*chip, 1 - c), me).wait_recv()
        for cp in first + passed:
            cp.wait_send()
        mine.wait()

    return pl.pallas_call(
        body, name=name, in_specs=[ANY], out_specs=ANY,
        out_shape=jax.ShapeDtypeStruct((N_DEV, r, cdim), src.dtype),
        scratch_shapes=[pltpu.SemaphoreType.DMA((7,)), pltpu.SemaphoreType.DMA((7,)),
                        pltpu.SemaphoreType.DMA(())],
    )(src)


def _swap_with_sibling(buf, name):
    _, nchip, r, cdim = buf.shape

    def body(g_ref, rcv_ref, send_sems, recv_sems):
        x, y, c, _ = _place()
        copies = [pltpu.make_async_remote_copy(
            src_ref=g_ref.at[1 - c, k], dst_ref=rcv_ref.at[k], send_sem=send_sems.at[k],
            recv_sem=recv_sems.at[k], device_id=(x, y, 1 - c), device_id_type=MESH_ID)
            for k in range(nchip)]
        for cp in copies:
            cp.start()
        for cp in copies:
            cp.wait()

    return pl.pallas_call(
        body, name=name, in_specs=[ANY], out_specs=ANY,
        out_shape=jax.ShapeDtypeStruct((nchip, r, cdim), buf.dtype),
        scratch_shapes=[pltpu.SemaphoreType.DMA((nchip,)), pltpu.SemaphoreType.DMA((nchip,))],
    )(buf)


def _exchange_chips(buf, name):
    nchip, r, cdim = buf.shape

    def body(b_ref, rcv_ref, send_sems, recv_sems, local_sem):
        x, y, c, chips = _place()
        my_chip = 2 * x + y
        mine = pltpu.make_async_copy(b_ref.at[my_chip], rcv_ref.at[my_chip], local_sem)
        mine.start()
        copies = [pltpu.make_async_remote_copy(
            src_ref=b_ref.at[2 * px + py], dst_ref=rcv_ref.at[my_chip], send_sem=send_sems.at[j],
            recv_sem=recv_sems.at[j], device_id=(px, py, c), device_id_type=MESH_ID)
            for j, (px, py) in enumerate(chips)]
        for cp in copies:
            cp.start()
        for j, (px, py) in enumerate(chips):
            pltpu.make_async_remote_copy(
                src_ref=b_ref.at[my_chip], dst_ref=rcv_ref.at[2 * px + py],
                send_sem=send_sems.at[j], recv_sem=recv_sems.at[j], device_id=(px, py, c),
                device_id_type=MESH_ID).wait_recv()
        for cp in copies:
            cp.wait_send()
        mine.wait()

    return pl.pallas_call(
        body, name=name, in_specs=[ANY], out_specs=ANY,
        out_shape=jax.ShapeDtypeStruct((nchip, r, cdim), buf.dtype),
        scratch_shapes=[pltpu.SemaphoreType.DMA((3,)), pltpu.SemaphoreType.DMA((3,)),
                        pltpu.SemaphoreType.DMA(())],
    )(buf)


def _pair_add(buf, rcv, core, name):
    _, nchip, r, cdim = buf.shape
    tr = _pick(r, (FLAT_ROWS, 8))

    def body(c_ref, a_ref, b_ref, o_ref):
        o_ref[...] = (a_ref[...] + b_ref[...]).astype(BF16)

    grid_spec = pltpu.PrefetchScalarGridSpec(
        num_scalar_prefetch=1, grid=(nchip, r // tr),
        in_specs=[pl.BlockSpec((None, None, tr, cdim), lambda k, i, c_ref: (c_ref[0], k, i, 0)),
                  pl.BlockSpec((None, tr, cdim), lambda k, i, c_ref: (k, i, 0))],
        out_specs=pl.BlockSpec((None, tr, cdim), lambda k, i, c_ref: (k, i, 0)))
    return pl.pallas_call(
        body, name=name, grid_spec=grid_spec,
        out_shape=jax.ShapeDtypeStruct((nchip, r, cdim), BF16),
        compiler_params=_params("parallel", "parallel"))(core, buf, rcv)


def _adamw(parts, w, m, v, name):
    npart, r, cdim = parts.shape
    tr = _pick(r, (FLAT_ROWS, 64, 32, 8))
    c1 = 1.0 - ADAM_B1 ** ADAM_STEP
    c2 = 1.0 - ADAM_B2 ** ADAM_STEP

    def body(p_ref, w_ref, m_ref, v_ref, g_ref, d_ref, mo_ref, vo_ref):
        g = p_ref[0].astype(F32)
        for k in range(1, npart):
            g = g + p_ref[k].astype(F32)
        mn = ADAM_B1 * m_ref[...] + (1.0 - ADAM_B1) * g
        vn = ADAM_B2 * v_ref[...] + (1.0 - ADAM_B2) * (g * g)
        g_ref[...] = g
        mo_ref[...] = mn
        vo_ref[...] = vn
        d_ref[...] = -ADAM_LR * ((mn / c1) / (jnp.sqrt(vn / c2) + ADAM_EPS) + ADAM_WD * w_ref[...])

    row = pl.BlockSpec((tr, cdim), lambda i: (i, 0))
    return pl.pallas_call(
        body, name=name, grid=(r // tr,),
        in_specs=[pl.BlockSpec((npart, tr, cdim), lambda i: (0, i, 0)), row, row, row],
        out_specs=[row] * 4, out_shape=[jax.ShapeDtypeStruct((r, cdim), F32)] * 4,
        compiler_params=_params("parallel"))(parts, w, m, v)


def _piece_rows(shape):
    return shape[0] * _round_up(int(math.prod(shape[1:])), LANES) // LANES


def _to_rows(a, nlead):
    lead, depth = a.shape[:nlead], a.shape[nlead]
    per = int(math.prod(a.shape[nlead + 1:]))
    if per % LANES:
        a = jnp.pad(a.reshape(lead + (depth, per)),
                    [(0, 0)] * (nlead + 1) + [(0, _round_up(per, LANES) - per)])
    return a.reshape(lead + (-1, LANES))


def _pack(arrays, row_mult, nlead=0):
    rows = jnp.concatenate([_to_rows(a, nlead) for a in arrays], axis=nlead)
    pad = _round_up(rows.shape[nlead], row_mult) - rows.shape[nlead]
    return jnp.pad(rows, [(0, 0)] * nlead + [(0, pad), (0, 0)])


def _unpack(rows, shapes, nlead=0):
    lead = rows.shape[:nlead]
    out, r = [], 0
    for shp in shapes:
        n = _piece_rows(shp)
        piece = lax.slice_in_dim(rows, r, r + n, axis=nlead)
        per = int(math.prod(shp[1:]))
        if per % LANES:
            piece = piece.reshape(lead + (shp[0], -1))[..., :per]
        out.append(piece.reshape(lead + tuple(shp)))
        r += n
    return out


def _full_weight(piece, name, layer):
    blk = piece[:, layer]
    return jnp.concatenate([blk[dev] for dev in range(N_DEV)],
                           axis=1 if name in COL_SHARDED else 0)


def _by_destination(name, grads):
    g = jnp.stack(grads)
    if name in COL_SHARDED:
        cs = g.shape[2] // N_DEV
        return jnp.stack([g[:, :, dev * cs:(dev + 1) * cs] for dev in range(N_DEV)])
    rs = g.shape[1] // N_DEV
    return jnp.stack([g[:, dev * rs:(dev + 1) * rs] for dev in range(N_DEV)])


def _heads(a):
    s = a.shape[0]
    return a.reshape(s, FOX_HEADS, FOX_DH).transpose(1, 0, 2)


def _unheads(a):
    return a.transpose(1, 0, 2).reshape(a.shape[1], FOX_W)


def kernel(*args):
    p = dict(zip(INPUTS, args))
    x0 = p["x"][0]
    mem = p["mem"][0]
    tgt = p["loss_target"][0]
    s, d = x0.shape
    core = lax.axis_index("c").astype(jnp.int32).reshape(1)

    mm_names = tuple(n for n in SHARDED if n != "conv_w")
    shard_shapes = {n: p[n].shape for n in SHARDED}
    gathered = _allgather(_pack([p[n].astype(BF16) for n in mm_names], 8), "gather_weights")
    pieces = dict(zip(mm_names, _unpack(gathered, [shard_shapes[n] for n in mm_names], nlead=1)))
    conv_g = _allgather(_pack([p["conv_w"]], 8), "gather_conv_w")
    conv_piece = _unpack(conv_g, [shard_shapes["conv_w"]], nlead=1)[0]

    def weights_of(layer):
        w = {n: _full_weight(pieces[n], n, layer) for n in mm_names}
        w_in = w.pop("w_in")
        fpad = jnp.pad(w_in[:, OFF_F:OFF_GP], ((0, 0), (0, ZR_W - ZR_F - FOX_HEADS)))
        w["w_cat"] = jnp.concatenate(
            [w_in[:, OFF_Q:OFF_F], w_in[:, OFF_GP:OFF_GF], w_in[:, OFF_GF:IN_W],
             w_in[:, 0:OFF_Q], fpad], axis=1)
        w["conv_w"] = _full_weight(conv_piece, "conv_w", layer)
        w["conv_b"] = p["conv_b"][layer].reshape(1, -1)
        w["pool_w"] = p["pool_w"][layer].astype(BF16)
        w["pool_scale"] = p["pool_scale"][layer].reshape(1, -1)
        w["b128"] = jnp.pad(p["b_forget"][layer], (0, 128 - FOX_HEADS)).reshape(1, 128)
        return w

    saved = []
    _, h1 = _norm_fwd(x0, None, None, p["mix_pre_g"][0], "norm_first")
    x_in = x0
    y_final = None
    for l in range(DEPTH):
        w = weights_of(l)
        sv = {"w": w, "x0": x_in, "h1": h1}
        zqkv = _mm(h1, w["w_cat"][:, :QKV_W], "nn", BF16, "mm_qkv")
        zr = _mm(h1, w["w_cat"][:, QKV_W:], "nn", F32, "mm_zr")
        c = _forget_fwd(zr, w["b128"], "forget_fwd")
        pm = _pool_fwd(zr, w["pool_w"], w["pool_scale"], "pool_fwd")
        qh, kh, vh = (_heads(zqkv[:, k * FOX_W:(k + 1) * FOX_W]) for k in range(3))
        c_t = c[:, :FOX_HEADS].T
        c_col, c_row = c_t[:, :, None], c_t[:, None, :]
        oh, lse = _fox_fwd(qh, kh, vh, c_col, c_row, "fox_fwd")
        o = _unheads(oh).astype(BF16)
        yp = _mm(pm, w["w_pool_br"], "nn", F32, "mm_pool_br")
        yf = _mm(o, w["w_fox_br"], "nn", F32, "mm_fox_br")
        merged = _gate_fwd(zr, yp, yf, "gate_fwd")
        f1 = _mm(merged, w["w_mix_out"], "nn", F32, "mm_mix_out")
        x1, h2 = _norm_fwd(x_in, f1, p["mix_post_g"][l], p["xa_pre_g"][l], "norm_mix_xa")
        _, mem_n = _norm_fwd(mem, None, None, p["mem_g"][l], "norm_mem")
        q2 = _mm(h2, w["w_xq"], "nn", BF16, "mm_xq")
        kv = _mm(mem_n, w["w_xkv"], "nn", BF16, "mm_xkv")
        o2 = _xattn_fwd(q2, kv, "xattn_fwd")
        f2 = _mm(o2, w["w_xo"], "nn", F32, "mm_xo")
        x2, h3 = _norm_fwd(x1, f2, p["xa_post_g"][l], p["ffn_pre_g"][l], "norm_xa_ffn")
        z3 = _mm(h3, w["w_up"], "nn", F32, "mm_up")
        a = _convglu_fwd(z3, w["conv_w"], w["conv_b"], "convglu_fwd")
        f3 = _mm(a, w["w_down"], "nn", F32, "mm_down")
        if l + 1 < DEPTH:
            x3, h1 = _norm_fwd(x2, f3, p["ffn_post_g"][l], p["mix_pre_g"][l + 1], "norm_ffn_mix")
        else:
            x3, _ = _norm_fwd(x2, f3, p["ffn_post_g"][l], None, "norm_last")
            y_final = x3
        sv.update(zr=zr, qh=qh, kh=kh, vh=vh, c_col=c_col, c_row=c_row, oh=oh, lse=lse, o=o,
                  pm=pm, yp=yp, yf=yf, merged=merged, f1=f1, x1=x1, h2=h2, mem_n=mem_n, q2=q2,
                  kv=kv, o2=o2, f2=f2, x2=x2, h3=h3, z3=z3, a=a, f3=f3)
        saved.append(sv)
        x_in = x3

    top = _norm_bwd("loss_head", y=y_final, tgt=tgt, f_prev=saved[-1]["f3"],
                    g_post=p["ffn_post_g"][DEPTH - 1])
    loss = lax.psum(top["loss"][0, 0], MESH_AXES)
    dres, df3 = top["dx"], top["df"]
    gw = {n: [None] * DEPTH for n in WEIGHTS}
    gw["ffn_post_g"][DEPTH - 1] = top["dg_post"]
    for l in reversed(range(DEPTH)):
        sv = saved[l]
        w = sv["w"]
        gw["w_down"][l] = _mm(sv["a"], df3, "tn", F32, "mm_d_w_down")
        da = _mm(df3, w["w_down"], "nt", BF16, "mm_d_a")
        dgc, duc, dcwg, dcwu, dcbg, dcbu = _convglu_bwd_pre(sv["z3"], da, w["conv_w"],
                                                            w["conv_b"], "convglu_bwd_pre")
        gw["conv_w"][l] = jnp.concatenate([dcwg, dcwu], axis=1)
        gw["conv_b"][l] = jnp.concatenate([dcbg, dcbu], axis=1)
        dz3 = jnp.concatenate(
            [_convglu_bwd_post(dgc, w["conv_w"], 0, "convglu_bwd_post_g"),
             _convglu_bwd_post(duc, w["conv_w"], D_FF // 256, "convglu_bwd_post_u")], axis=1)
        gw["w_up"][l] = _mm(sv["h3"], dz3, "tn", F32, "mm_d_w_up")
        dh3 = _mm(dz3, w["w_up"], "nt", F32, "mm_d_h3")
        nb = _norm_bwd("norm_bwd_ffn_xa", dh=dh3, x=sv["x2"], g_pre=p["ffn_pre_g"][l], dres=dres,
                       f_prev=sv["f2"], g_post=p["xa_post_g"][l])
        gw["ffn_pre_g"][l], gw["xa_post_g"][l] = nb["dg_pre"], nb["dg_post"]
        dres, df2 = nb["dx"], nb["df"]
        gw["w_xo"][l] = _mm(sv["o2"], df2, "tn", F32, "mm_d_w_xo")
        do2 = _mm(df2, w["w_xo"], "nt", BF16, "mm_d_o2")
        dq2, dk2, dv2 = _xattn_bwd(sv["q2"], sv["kv"], do2, "xattn_bwd")
        dkv = jnp.concatenate([dk2, dv2], axis=1)
        gw["w_xq"][l] = _mm(sv["h2"], dq2, "tn", F32, "mm_d_w_xq")
        dh2 = _mm(dq2, w["w_xq"], "nt", F32, "mm_d_h2")
        gw["w_xkv"][l] = _mm(sv["mem_n"], dkv, "tn", F32, "mm_d_w_xkv")
        dmem_n = _mm(dkv, w["w_xkv"], "nt", F32, "mm_d_mem")
        gw["mem_g"][l] = _rms_dg(mem, p["mem_g"][l], dmem_n, "norm_mem_bwd")
        nb = _norm_bwd("norm_bwd_xa_mix", dh=dh2, x=sv["x1"], g_pre=p["xa_pre_g"][l], dres=dres,
                       f_prev=sv["f1"], g_post=p["mix_post_g"][l])
        gw["xa_pre_g"][l], gw["mix_post_g"][l] = nb["dg_pre"], nb["dg_post"]
        dres, df1 = nb["dx"], nb["df"]
        gw["w_mix_out"][l] = _mm(sv["merged"], df1, "tn", F32, "mm_d_w_mix_out")
        dmerged = _mm(df1, w["w_mix_out"], "nt", F32, "mm_d_merged")
        dyp, dyf, dzg = _gate_bwd(dmerged, sv["zr"], sv["yp"], sv["yf"], "gate_bwd")
        gw["w_pool_br"][l] = _mm(sv["pm"], dyp, "tn", F32, "mm_d_w_pool_br")
        dpm = _mm(dyp, w["w_pool_br"], "nt", F32, "mm_d_pm")
        gw["w_fox_br"][l] = _mm(sv["o"], dyf, "tn", F32, "mm_d_w_fox_br")
        do = _mm(dyf, w["w_fox_br"], "nt", BF16, "mm_d_o")
        du, dpw, dsc = _pool_bwd(sv["zr"], dpm, w["pool_w"], w["pool_scale"], "pool_bwd")
        gw["pool_w"][l], gw["pool_scale"][l] = dpw, dsc
        dqh, dkh, dvh, dc = _fox_bwd(sv["qh"], sv["kh"], sv["vh"], sv["c_col"], sv["c_row"],
                                     sv["oh"], sv["lse"], _heads(do), "fox_bwd")
        dc_rows = jnp.pad(dc[:, 0, :].T, ((0, 0), (0, 128 - FOX_HEADS)))
        dzf, db = _forget_bwd(dc_rows, sv["zr"], w["b128"], "forget_bwd")
        gw["b_forget"][l] = db[0, :FOX_HEADS]
        dzf_pad = jnp.pad(dzf[:, :FOX_HEADS], ((0, 0), (0, ZR_W - ZR_F - FOX_HEADS)))
        dzc = jnp.concatenate([_unheads(dqh).astype(BF16), _unheads(dkh).astype(BF16),
                               _unheads(dvh).astype(BF16), dzg, du, dzf_pad.astype(BF16)], axis=1)
        dw_cat = _mm(sv["h1"], dzc, "tn", F32, "mm_d_w_in")
        gw["w_in"][l] = jnp.concatenate(
            [dw_cat[:, QKV_W + ZR_U:QKV_W + ZR_F], dw_cat[:, :QKV_W],
             dw_cat[:, QKV_W + ZR_F:QKV_W + ZR_F + FOX_HEADS],
             dw_cat[:, QKV_W + ZR_GP:QKV_W + ZR_U]], axis=1)
        dh1 = _mm(dzc, w["w_cat"], "nt", F32, "mm_d_h1")
        if l > 0:
            nb = _norm_bwd("norm_bwd_mix_ffn", dh=dh1, x=sv["x0"], g_pre=p["mix_pre_g"][l],
                           dres=dres, f_prev=saved[l - 1]["f3"], g_post=p["ffn_post_g"][l - 1])
            gw["ffn_post_g"][l - 1] = nb["dg_post"]
            df3 = nb["df"]
        else:
            nb = _norm_bwd("norm_bwd_first", dh=dh1, x=sv["x0"], g_pre=p["mix_pre_g"][l], dres=dres)
        gw["mix_pre_g"][l] = nb["dg_pre"]
        dres = nb["dx"]
    grad_x = dres[None]

    by_dest = _pack([_by_destination(n, gw[n]) for n in SHARDED], FLAT_ROWS, nlead=1)
    buf = by_dest.reshape(4, 2, -1, LANES).transpose(1, 0, 2, 3)
    rcv = _swap_with_sibling(buf, "grads_to_sibling")
    pair = _pair_add(buf, rcv, core, "grads_pair_add")
    parts = _exchange_chips(pair, "grads_to_chips")
    shapes = [shard_shapes[n] for n in SHARDED]
    packed = [_pack([p[pre + n] for n in SHARDED], FLAT_ROWS) for pre in ("", "m_", "v_")]
    res_sh = [dict(zip(SHARDED, _unpack(r, shapes)))
              for r in _adamw(parts, *packed, "adamw_sharded")]

    rep_shapes = [p[n].shape for n in REPLICATED]
    part = _pack([jnp.stack(gw[n]).reshape(p[n].shape) for n in REPLICATED], 8)
    allparts = _allgather(part, "gather_small_grads")
    packed = [_pack([p[pre + n] for n in REPLICATED], 8) for pre in ("", "m_", "v_")]
    res_rep = [dict(zip(REPLICATED, _unpack(r, rep_shapes)))
               for r in _adamw(allparts, *packed, "adamw_replicated")]

    outs = [loss, grad_x]
    for k in range(4):
        outs += [res_sh[k][n] if n in SHARDED else res_rep[k][n] for n in WEIGHTS]
    return tuple(outs)
```

```python
import functools
import math

import jax
import jax.numpy as jnp
from jax import lax
from jax.experimental import pallas as pl
from jax.experimental.pallas import tpu as pltpu

F32 = jnp.float32
BF16 = jnp.bfloat16
MESH_AXES = ("x", "y", "c")
N_DEV = 8
MESH_ID = pl.DeviceIdType.MESH

DEPTH = 4
POOL_WINDOWS = (2, 4, 8, 16)
POOL_GROUP = 128
POOL_W = 512
POOL_HALO = 16
FOX_HEADS = 8
FOX_DH = 64
FOX_W = 512
X_HEADS = 4
X_DH = 128
X_W = 512
D_FF = 2816
RMS_EPS = 1e-6
ADAM_LR, ADAM_B1, ADAM_B2, ADAM_EPS, ADAM_WD, ADAM_STEP = 0.001, 0.9, 0.999, 1e-08, 0.01, 10

OFF_Q, OFF_F, OFF_GP, OFF_GF, IN_W = 512, 2048, 2056, 3080, 4104
ZR_GP, ZR_GF, ZR_U, ZR_F, ZR_W = 0, 1024, 2048, 2560, 3072
QKV_W = 3 * FOX_W
ZC_W = QKV_W + ZR_W

LANES = 1024
FLAT_ROWS = 128

SHARDED = ("w_in", "w_pool_br", "w_fox_br", "w_mix_out", "w_xq", "w_xkv", "w_xo", "w_up",
           "conv_w", "w_down")
COL_SHARDED = ("w_in", "w_pool_br", "w_fox_br", "w_xo", "w_up", "conv_w")
REPLICATED = ("mix_pre_g", "mix_post_g", "b_forget", "pool_w", "pool_scale", "xa_pre_g",
              "xa_post_g", "mem_g", "ffn_pre_g", "ffn_post_g", "conv_b")
WEIGHTS = ("mix_pre_g", "mix_post_g", "w_in", "b_forget", "pool_w", "pool_scale", "w_pool_br",
           "w_fox_br", "w_mix_out", "xa_pre_g", "xa_post_g", "mem_g", "w_xq", "w_xkv", "w_xo",
           "ffn_pre_g", "ffn_post_g", "w_up", "conv_w", "conv_b", "w_down")
INPUTS = (("x", "mem") + WEIGHTS + ("loss_target",) + tuple("m_" + n for n in WEIGHTS)
          + tuple("v_" + n for n in WEIGHTS))


def _pick(n, prefs):
    for p in prefs:
        if n % p == 0:
            return p
    return n


def _ktile(k):
    if k <= 2816:
        return k
    return _pick(k, (2816, 2048, 1536, 1024, 512))


def _round_up(n, m):
    return (n + m - 1) // m * m


def _params(*sem):
    return pltpu.CompilerParams(dimension_semantics=sem)


_DIMS = {"nn": (((1,), (0,)), ((), ())), "nt": (((1,), (1,)), ((), ())),
         "tn": (((0,), (0,)), ((), ()))}


def _mm(a, b, mode, out_dtype, name):
    if mode == "nn":
        (m, k), (_, n) = a.shape, b.shape
    elif mode == "nt":
        (m, k), (n, _) = a.shape, b.shape
    else:
        (k, m), (_, n) = a.shape, b.shape
    tm = _pick(m, (1024, 512, 256, 128))
    tn = _pick(n, (512, 256, 128))
    tk = _ktile(k)
    nk = k // tk

    def body(a_ref, b_ref, o_ref, acc_ref):
        kk = pl.program_id(2)

        @pl.when(kk == 0)
        def _():
            acc_ref[...] = jnp.zeros_like(acc_ref)

        acc_ref[...] += lax.dot_general(a_ref[...].astype(BF16), b_ref[...].astype(BF16),
                                        _DIMS[mode], preferred_element_type=F32)

        @pl.when(kk == nk - 1)
        def _():
            o_ref[...] = acc_ref[...].astype(out_dtype)

    if mode == "tn":
        a_spec = pl.BlockSpec((tk, tm), lambda i, j, kk: (kk, i))
    else:
        a_spec = pl.BlockSpec((tm, tk), lambda i, j, kk: (i, kk))
    if mode == "nt":
        b_spec = pl.BlockSpec((tn, tk), lambda i, j, kk: (j, kk))
    else:
        b_spec = pl.BlockSpec((tk, tn), lambda i, j, kk: (kk, j))
    return pl.pallas_call(
        body, name=name, grid=(m // tm, n // tn, nk),
        in_specs=[a_spec, b_spec],
        out_specs=pl.BlockSpec((tm, tn), lambda i, j, kk: (i, j)),
        out_shape=jax.ShapeDtypeStruct((m, n), out_dtype),
        scratch_shapes=[pltpu.VMEM((tm, tn), F32)],
        compiler_params=_params("parallel", "parallel", "arbitrary"),
    )(a, b)


def _rms(x, g):
    r = lax.rsqrt(jnp.mean(x * x, axis=-1, keepdims=True) + RMS_EPS)
    return x * r * g


def _rms_bwd(x, g, dy):
    r = lax.rsqrt(jnp.mean(x * x, axis=-1, keepdims=True) + RMS_EPS)
    xh = x * r
    t = dy * g
    dx = r * (t - xh * jnp.mean(t * xh, axis=-1, keepdims=True))
    dg = jnp.sum(dy * xh, axis=0, keepdims=True)
    return dx, dg


def _norm_fwd(x_in, f, g_post, g_pre, name):
    s, d = x_in.shape
    tm = _pick(s, (512, 256, 128))
    has_post, has_pre = f is not None, g_pre is not None
    row = pl.BlockSpec((tm, d), lambda i: (i, 0))
    vec = pl.BlockSpec((1, d), lambda i: (0, 0))

    def body(*refs):
        refs = list(refs)
        x = refs.pop(0)[...]
        if has_post:
            fv = refs.pop(0)[...]
            x = x + _rms(fv, refs.pop(0)[...])
        gpre = refs.pop(0)[...] if has_pre else None
        if has_post:
            refs.pop(0)[...] = x
        if has_pre:
            refs.pop(0)[...] = _rms(x, gpre).astype(BF16)

    ins, specs, outs, ospecs = [x_in], [row], [], []
    if has_post:
        ins += [f, g_post.reshape(1, d)]
        specs += [row, vec]
        outs.append(jax.ShapeDtypeStruct((s, d), F32))
        ospecs.append(row)
    if has_pre:
        ins.append(g_pre.reshape(1, d))
        specs.append(vec)
        outs.append(jax.ShapeDtypeStruct((s, d), BF16))
        ospecs.append(row)
    res = pl.pallas_call(body, name=name, grid=(s // tm,), in_specs=specs, out_specs=ospecs,
                         out_shape=outs, compiler_params=_params("parallel"))(*ins)
    res = list(res)
    x_out = res.pop(0) if has_post else None
    h = res.pop(0) if has_pre else None
    return x_out, h


def _norm_bwd(name, *, dh=None, x=None, g_pre=None, dres=None, y=None, tgt=None, f_prev=None,
              g_post=None):
    top = y is not None
    has_post = f_prev is not None
    ref_arr = y if top else x
    s, d = ref_arr.shape
    tm = _pick(s, (512, 256, 128))
    row = pl.BlockSpec((tm, d), lambda i: (i, 0))
    vec = pl.BlockSpec((1, d), lambda i: (0, 0))
    one = pl.BlockSpec((1, 1), lambda i: (0, 0))

    def body(*refs):
        refs = list(refs)
        i = pl.program_id(0)
        if top:
            yv, tv = refs.pop(0)[...], refs.pop(0)[...]
        else:
            dhv, xv, gv, dr = (refs.pop(0)[...].astype(F32), refs.pop(0)[...], refs.pop(0)[...],
                               refs.pop(0)[...])
        if has_post:
            fv, gp = refs.pop(0)[...], refs.pop(0)[...]
        dx_ref = refs.pop(0)
        if top:
            loss_ref = refs.pop(0)
        else:
            dgpre_ref = refs.pop(0)
        if has_post:
            df_ref, dgpost_ref = refs.pop(0), refs.pop(0)

        if top:
            err = yv - tv
            dx = err * (1.0 / d)
            part = 0.5 * jnp.sum(jnp.sum(err * err, axis=-1, keepdims=True) * (1.0 / d),
                                 axis=0, keepdims=True)
        else:
            dxn, dgpre = _rms_bwd(xv, gv, dhv)
            dx = dr + dxn
        dx_ref[...] = dx
        if has_post:
            df, dgpost = _rms_bwd(fv, gp, dx)
            df_ref[...] = df.astype(BF16)

        @pl.when(i == 0)
        def _():
            if top:
                loss_ref[...] = jnp.zeros_like(loss_ref)
            else:
                dgpre_ref[...] = jnp.zeros_like(dgpre_ref)
            if has_post:
                dgpost_ref[...] = jnp.zeros_like(dgpost_ref)

        if top:
            loss_ref[...] += part
        else:
            dgpre_ref[...] += dgpre
        if has_post:
            dgpost_ref[...] += dgpost

    if top:
        ins, specs = [y, tgt], [row, row]
    else:
        ins, specs = [dh, x, g_pre.reshape(1, d), dres], [row, row, vec, row]
    if has_post:
        ins += [f_prev, g_post.reshape(1, d)]
        specs += [row, vec]
    outs, ospecs = [jax.ShapeDtypeStruct((s, d), F32)], [row]
    if top:
        outs.append(jax.ShapeDtypeStruct((1, 1), F32))
        ospecs.append(one)
    else:
        outs.append(jax.ShapeDtypeStruct((1, d), F32))
        ospecs.append(vec)
    if has_post:
        outs += [jax.ShapeDtypeStruct((s, d), BF16), jax.ShapeDtypeStruct((1, d), F32)]
        ospecs += [row, vec]
    res = list(pl.pallas_call(body, name=name, grid=(s // tm,), in_specs=specs,
                              out_specs=ospecs, out_shape=outs,
                              compiler_params=_params("arbitrary"))(*ins))
    out = {"dx": res.pop(0)}
    out["loss" if top else "dg_pre"] = res.pop(0)
    if has_post:
        out["df"], out["dg_post"] = res.pop(0), res.pop(0)
    return out


def _rms_dg(x, g, dy, name):
    s, d = x.shape
    tm = _pick(s, (256, 128))
    row = pl.BlockSpec((tm, d), lambda i: (i, 0))
    vec = pl.BlockSpec((1, d), lambda i: (0, 0))

    def body(x_ref, g_ref, dy_ref, dg_ref):
        @pl.when(pl.program_id(0) == 0)
        def _():
            dg_ref[...] = jnp.zeros_like(dg_ref)

        dg_ref[...] += _rms_bwd(x_ref[...], g_ref[...], dy_ref[...])[1]

    return pl.pallas_call(body, name=name, grid=(s // tm,), in_specs=[row, vec, row],
                          out_specs=vec, out_shape=jax.ShapeDtypeStruct((1, d), F32),
                          compiler_params=_params("arbitrary"))(x, g.reshape(1, d), dy)


def _forget_fwd(zr, b128, name):
    s = zr.shape[0]
    tm = _pick(s, (256, 128))
    fblk = ZR_F // 128

    def body(z_ref, b_ref, c_ref, carry):
        @pl.when(pl.program_id(0) == 0)
        def _():
            carry[...] = jnp.zeros_like(carry)

        a = z_ref[...] + b_ref[...]
        acc = jnp.minimum(a, 0.0) - jnp.log1p(jnp.exp(-jnp.abs(a)))
        rows = lax.broadcasted_iota(jnp.int32, acc.shape, 0)
        k = 1
        while k < tm:
            acc = acc + jnp.where(rows >= k, pltpu.roll(acc, k, 0), 0.0)
            k *= 2
        acc = acc + carry[...]
        c_ref[...] = acc
        carry[...] = acc[tm - 1:tm, :]

    return pl.pallas_call(
        body, name=name, grid=(s // tm,),
        in_specs=[pl.BlockSpec((tm, 128), lambda i: (i, fblk)),
                  pl.BlockSpec((1, 128), lambda i: (0, 0))],
        out_specs=pl.BlockSpec((tm, 128), lambda i: (i, 0)),
        out_shape=jax.ShapeDtypeStruct((s, 128), F32),
        scratch_shapes=[pltpu.VMEM((1, 128), F32)],
        compiler_params=_params("arbitrary"))(zr, b128)


def _forget_bwd(dc, zr, b128, name):
    s = zr.shape[0]
    tm = _pick(s, (256, 128))
    nt = s // tm
    fblk = ZR_F // 128

    def body(dc_ref, z_ref, b_ref, dz_ref, db_ref, carry):
        @pl.when(pl.program_id(0) == 0)
        def _():
            carry[...] = jnp.zeros_like(carry)
            db_ref[...] = jnp.zeros_like(db_ref)

        acc = dc_ref[...]
        rows = lax.broadcasted_iota(jnp.int32, acc.shape, 0)
        k = 1
        while k < tm:
            acc = acc + jnp.where(rows < tm - k, pltpu.roll(acc, tm - k, 0), 0.0)
            k *= 2
        acc = acc + carry[...]
        carry[...] = acc[0:1, :]
        a = z_ref[...] + b_ref[...]
        dz = acc / (1.0 + jnp.exp(a))
        dz_ref[...] = dz
        db_ref[...] += jnp.sum(dz, axis=0, keepdims=True)

    return pl.pallas_call(
        body, name=name, grid=(nt,),
        in_specs=[pl.BlockSpec((tm, 128), lambda i: (nt - 1 - i, 0)),
                  pl.BlockSpec((tm, 128), lambda i: (nt - 1 - i, fblk)),
                  pl.BlockSpec((1, 128), lambda i: (0, 0))],
        out_specs=[pl.BlockSpec((tm, 128), lambda i: (nt - 1 - i, 0)),
                   pl.BlockSpec((1, 128), lambda i: (0, 0))],
        out_shape=[jax.ShapeDtypeStruct((s, 128), F32), jax.ShapeDtypeStruct((1, 128), F32)],
        scratch_shapes=[pltpu.VMEM((1, 128), F32)],
        compiler_params=_params("arbitrary"))(dc, zr, b128)


def _pooled(ext, t_abs, g, w):
    e = ext[:, g * POOL_GROUP:(g + 1) * POOL_GROUP]
    acc = e
    k = 1
    while k < w:
        acc = acc + pltpu.roll(acc, k, 0)
        k *= 2
    cnt = jnp.minimum(t_abs + 1, w).astype(F32)
    return acc[POOL_HALO:] / cnt - e[POOL_HALO:]


def _pool_specs(s, tm):
    per = tm // POOL_HALO
    ublk = ZR_U // POOL_W
    return [pl.BlockSpec((tm, POOL_W), lambda i: (i, ublk)),
            pl.BlockSpec((POOL_HALO, POOL_W), lambda i: (jnp.maximum(i * per - 1, 0), ublk))]


def _pool_fwd(zr, pw, scale, name):
    s = zr.shape[0]
    tm = _pick(s, (512, 256, 128))

    def body(u_ref, h_ref, pw_ref, sc_ref, o_ref):
        i = pl.program_id(0)
        halo = jnp.where(i > 0, h_ref[...], 0.0)
        ext = jnp.concatenate([halo, u_ref[...]], axis=0)
        t_abs = i * tm + lax.broadcasted_iota(jnp.int32, (tm, 1), 0)
        outs = []
        for g, w in enumerate(POOL_WINDOWS):
            pooled = _pooled(ext, t_abs, g, w)
            outs.append(jnp.dot(pooled.astype(BF16), pw_ref[g], preferred_element_type=F32))
        o_ref[...] = (jnp.concatenate(outs, axis=1) * sc_ref[...]).astype(BF16)

    return pl.pallas_call(
        body, name=name, grid=(s // tm,),
        in_specs=_pool_specs(s, tm) + [
            pl.BlockSpec((len(POOL_WINDOWS), POOL_GROUP, POOL_GROUP), lambda i: (0, 0, 0)),
            pl.BlockSpec((1, POOL_W), lambda i: (0, 0))],
        out_specs=pl.BlockSpec((tm, POOL_W), lambda i: (i, 0)),
        out_shape=jax.ShapeDtypeStruct((s, POOL_W), BF16),
        compiler_params=_params("parallel"))(zr, zr, pw, scale)


def _pool_bwd(zr, dpm, pw, scale, name):
    s = zr.shape[0]
    tm = _pick(s, (512, 256, 128))
    nt = s // tm
    per = tm // POOL_HALO
    n_ext = tm + POOL_HALO
    ng = len(POOL_WINDOWS)

    def body(u_ref, h_ref, d_ref, dn_ref, pw_ref, sc_ref, du_ref, dpw_ref, dsc_ref):
        i = pl.program_id(0)

        @pl.when(i == 0)
        def _():
            dpw_ref[...] = jnp.zeros_like(dpw_ref)
            dsc_ref[...] = jnp.zeros_like(dsc_ref)

        halo = jnp.where(i > 0, h_ref[...], 0.0)
        ext_u = jnp.concatenate([halo, u_ref[...]], axis=0)
        nxt = jnp.where(i < nt - 1, dn_ref[...], 0.0)
        ext_d = jnp.concatenate([d_ref[...], nxt], axis=0)
        t_abs = i * tm + lax.broadcasted_iota(jnp.int32, (tm, 1), 0)
        t_ext = i * tm + lax.broadcasted_iota(jnp.int32, (n_ext, 1), 0)
        dus, dscs = [], []
        for g, w in enumerate(POOL_WINDOWS):
            sl = slice(g * POOL_GROUP, (g + 1) * POOL_GROUP)
            pooled = _pooled(ext_u, t_abs, g, w).astype(BF16)
            mixed = jnp.dot(pooled, pw_ref[g], preferred_element_type=F32)
            d_g = ext_d[:, sl]
            dscs.append(jnp.sum(d_g[:tm] * mixed, axis=0, keepdims=True))
            dmixed = (d_g * sc_ref[:, sl]).astype(BF16)
            dpw_ref[g] += lax.dot_general(pooled, dmixed[:tm], _DIMS["tn"],
                                          preferred_element_type=F32)
            dpooled = lax.dot_general(dmixed, pw_ref[g], _DIMS["nt"], preferred_element_type=F32)
            acc = dpooled / jnp.minimum(t_ext + 1, w).astype(F32)
            k = 1
            while k < w:
                acc = acc + pltpu.roll(acc, n_ext - k, 0)
                k *= 2
            dus.append(acc[:tm] - dpooled[:tm])
        du_ref[...] = jnp.concatenate(dus, axis=1).astype(BF16)
        dsc_ref[...] += jnp.concatenate(dscs, axis=1)

    return pl.pallas_call(
        body, name=name, grid=(nt,),
        in_specs=_pool_specs(s, tm) + [
            pl.BlockSpec((tm, POOL_W), lambda i: (i, 0)),
            pl.BlockSpec((POOL_HALO, POOL_W),
                         lambda i: (jnp.minimum((i + 1) * per, s // POOL_HALO - 1), 0)),
            pl.BlockSpec((ng, POOL_GROUP, POOL_GROUP), lambda i: (0, 0, 0)),
            pl.BlockSpec((1, POOL_W), lambda i: (0, 0))],
        out_specs=[pl.BlockSpec((tm, POOL_W), lambda i: (i, 0)),
                   pl.BlockSpec((ng, POOL_GROUP, POOL_GROUP), lambda i: (0, 0, 0)),
                   pl.BlockSpec((1, POOL_W), lambda i: (0, 0))],
        out_shape=[jax.ShapeDtypeStruct((s, POOL_W), BF16),
                   jax.ShapeDtypeStruct((ng, POOL_GROUP, POOL_GROUP), F32),
                   jax.ShapeDtypeStruct((1, POOL_W), F32)],
        compiler_params=_params("arbitrary"))(zr, zr, dpm, dpm, pw, scale)


def _fox_logits(q, k, cq, ck, scale, diagonal):
    sc = lax.dot_general(q * scale, k, _DIMS["nt"], preferred_element_type=F32) + cq - ck
    if diagonal:
        rows = lax.broadcasted_iota(jnp.int32, sc.shape, 0)
        cols = lax.broadcasted_iota(jnp.int32, sc.shape, 1)
        sc = jnp.where(cols <= rows, sc, -jnp.inf)
    return sc


def _fox_fwd(q, k, v, c_col, c_row, name):
    h, s, dh = q.shape
    t = _pick(s, (512, 256, 128))
    nb = s // t
    scale = 1.0 / math.sqrt(dh)

    pairs = [(i, j) for i in range(nb) for j in range(i + 1)]
    qi_tab = jnp.asarray([pr[0] for pr in pairs], jnp.int32)
    kj_tab = jnp.asarray([pr[1] for pr in pairs], jnp.int32)

    def body(qi_ref, kj_ref, q_ref, k_ref, v_ref, cq_ref, ck_ref, o_ref, lse_ref, m_sc, l_sc,
             acc_sc):
        i, j = qi_ref[pl.program_id(1)], kj_ref[pl.program_id(1)]

        @pl.when(j == 0)
        def _():
            m_sc[...] = jnp.full_like(m_sc, -jnp.inf)
            l_sc[...] = jnp.zeros_like(l_sc)
            acc_sc[...] = jnp.zeros_like(acc_sc)

        def step(diagonal):
            sc = _fox_logits(q_ref[...], k_ref[...], cq_ref[...], ck_ref[...], scale, diagonal)
            m_new = jnp.maximum(m_sc[...], jnp.max(sc, axis=-1, keepdims=True))
            alpha = jnp.exp(m_sc[...] - m_new)
            p = jnp.exp(sc - m_new)
            l_sc[...] = alpha * l_sc[...] + jnp.sum(p, axis=-1, keepdims=True)
            p_hi = p.astype(BF16)
            p_lo = (p - p_hi.astype(F32)).astype(BF16)
            vv = v_ref[...]
            acc_sc[...] = (alpha * acc_sc[...] + jnp.dot(p_hi, vv, preferred_element_type=F32)
                           + jnp.dot(p_lo, vv, preferred_element_type=F32))
            m_sc[...] = m_new

        pl.when(j < i)(functools.partial(step, False))
        pl.when(j == i)(functools.partial(step, True))

        @pl.when(j == i)
        def _():
            o_ref[...] = acc_sc[...] / l_sc[...]
            lse_ref[...] = m_sc[...] + jnp.log(l_sc[...])

    qspec = pl.BlockSpec((None, t, dh), lambda hh, n, qi, kj: (hh, qi[n], 0))
    kspec = pl.BlockSpec((None, t, dh), lambda hh, n, qi, kj: (hh, kj[n], 0))
    colspec = pl.BlockSpec((None, t, 1), lambda hh, n, qi, kj: (hh, qi[n], 0))
    grid_spec = pltpu.PrefetchScalarGridSpec(
        num_scalar_prefetch=2, grid=(h, len(pairs)),
        in_specs=[qspec, kspec, kspec, colspec,
                  pl.BlockSpec((None, 1, t), lambda hh, n, qi, kj: (hh, 0, kj[n]))],
        out_specs=[qspec, colspec],
        scratch_shapes=[pltpu.VMEM((t, 1), F32), pltpu.VMEM((t, 1), F32),
                        pltpu.VMEM((t, dh), F32)])
    return pl.pallas_call(
        body, name=name, grid_spec=grid_spec,
        out_shape=[jax.ShapeDtypeStruct((h, s, dh), F32), jax.ShapeDtypeStruct((h, s, 1), F32)],
        compiler_params=_params("parallel", "arbitrary"))(qi_tab, kj_tab, q, k, v, c_col, c_row)


def _fox_bwd(q, k, v, c_col, c_row, o, lse, do, name):
    h, s, dh = q.shape
    t = _pick(s, (512, 256, 128))
    nb = s // t
    scale = 1.0 / math.sqrt(dh)

    pairs = [(i, j) for j in range(nb) for i in range(j, nb)]
    qi_tab = jnp.asarray([pr[0] for pr in pairs], jnp.int32)
    kj_tab = jnp.asarray([pr[1] for pr in pairs], jnp.int32)

    def body(qi_ref, kj_ref, q_ref, k_ref, v_ref, cq_ref, ck_ref, o_ref, lse_ref, do_ref,
             dq_ref, dk_ref, dv_ref, dc_ref, dk_sc, dv_sc, dc_sc):
        i, j = qi_ref[pl.program_id(1)], kj_ref[pl.program_id(1)]

        @pl.when(pl.program_id(1) == 0)
        def _():
            dq_ref[...] = jnp.zeros_like(dq_ref)

        @pl.when(i == j)
        def _():
            dk_sc[...] = jnp.zeros_like(dk_sc)
            dv_sc[...] = jnp.zeros_like(dv_sc)
            dc_sc[...] = jnp.zeros_like(dc_sc)

        def step(diagonal):
            qv, kv, dov = q_ref[...], k_ref[...], do_ref[...]
            sc = _fox_logits(qv, kv, cq_ref[...], ck_ref[...], scale, diagonal)
            p = jnp.exp(sc - lse_ref[...])
            delta = jnp.sum(dov.astype(F32) * o_ref[...], axis=-1, keepdims=True)
            dv_sc[...] += lax.dot_general(p.astype(BF16), dov, _DIMS["tn"],
                                          preferred_element_type=F32)
            dp = lax.dot_general(dov, v_ref[...], _DIMS["nt"], preferred_element_type=F32)
            ds = p * (dp - delta)
            dc_sc[...] += jnp.sum(ds, axis=0, keepdims=True)
            dsb = ds.astype(BF16)
            dk_sc[...] += scale * lax.dot_general(dsb, qv, _DIMS["tn"],
                                                  preferred_element_type=F32)
            rows = pl.ds(pl.multiple_of(i * t, t), t)
            dq_ref[rows, :] += scale * jnp.dot(dsb, kv, preferred_element_type=F32)

        pl.when(i > j)(functools.partial(step, False))
        pl.when(i == j)(functools.partial(step, True))

        @pl.when(i == nb - 1)
        def _():
            dk_ref[...] = dk_sc[...]
            dv_ref[...] = dv_sc[...]
            dc_ref[...] = -dc_sc[...]

    qspec = pl.BlockSpec((None, t, dh), lambda hh, n, qi, kj: (hh, qi[n], 0))
    kspec = pl.BlockSpec((None, t, dh), lambda hh, n, qi, kj: (hh, kj[n], 0))
    qcol = pl.BlockSpec((None, t, 1), lambda hh, n, qi, kj: (hh, qi[n], 0))
    krow = pl.BlockSpec((None, 1, t), lambda hh, n, qi, kj: (hh, 0, kj[n]))
    grid_spec = pltpu.PrefetchScalarGridSpec(
        num_scalar_prefetch=2, grid=(h, len(pairs)),
        in_specs=[qspec, kspec, kspec, qcol, krow, qspec, qcol, qspec],
        out_specs=[pl.BlockSpec((None, s, dh), lambda hh, n, qi, kj: (hh, 0, 0)), kspec, kspec,
                   krow],
        scratch_shapes=[pltpu.VMEM((t, dh), F32), pltpu.VMEM((t, dh), F32),
                        pltpu.VMEM((1, t), F32)])
    return pl.pallas_call(
        body, name=name, grid_spec=grid_spec,
        out_shape=[jax.ShapeDtypeStruct((h, s, dh), F32), jax.ShapeDtypeStruct((h, s, dh), F32),
                   jax.ShapeDtypeStruct((h, s, dh), F32), jax.ShapeDtypeStruct((h, 1, s), F32)],
        compiler_params=_params("parallel", "arbitrary"))(
            qi_tab, kj_tab, q, k, v, c_col, c_row, o, lse, do)


def _gate_fwd(zr, yp, yf, name):
    s, d = yp.shape
    tm = _pick(s, (256, 128))
    row = pl.BlockSpec((tm, d), lambda i: (i, 0))

    def body(zp_ref, zf_ref, yp_ref, yf_ref, o_ref):
        o_ref[...] = (jax.nn.sigmoid(zp_ref[...]) * yp_ref[...]
                      + jax.nn.sigmoid(zf_ref[...]) * yf_ref[...]).astype(BF16)

    return pl.pallas_call(
        body, name=name, grid=(s // tm,),
        in_specs=[pl.BlockSpec((tm, d), lambda i: (i, ZR_GP // d)),
                  pl.BlockSpec((tm, d), lambda i: (i, ZR_GF // d)), row, row],
        out_specs=row, out_shape=jax.ShapeDtypeStruct((s, d), BF16),
        compiler_params=_params("parallel"))(zr, zr, yp, yf)


def _gate_bwd(dm, zr, yp, yf, name):
    s, d = yp.shape
    tm = _pick(s, (256, 128))
    row = pl.BlockSpec((tm, d), lambda i: (i, 0))

    def body(dm_ref, zp_ref, zf_ref, yp_ref, yf_ref, dyp_ref, dyf_ref, dz_ref):
        dmv = dm_ref[...]
        gp, gf = jax.nn.sigmoid(zp_ref[...]), jax.nn.sigmoid(zf_ref[...])
        dyp_ref[...] = (dmv * gp).astype(BF16)
        dyf_ref[...] = (dmv * gf).astype(BF16)
        dz_ref[:, :d] = (dmv * yp_ref[...] * gp * (1.0 - gp)).astype(BF16)
        dz_ref[:, d:] = (dmv * yf_ref[...] * gf * (1.0 - gf)).astype(BF16)

    return pl.pallas_call(
        body, name=name, grid=(s // tm,),
        in_specs=[row, pl.BlockSpec((tm, d), lambda i: (i, ZR_GP // d)),
                  pl.BlockSpec((tm, d), lambda i: (i, ZR_GF // d)), row, row],
        out_specs=[row, row, pl.BlockSpec((tm, 2 * d), lambda i: (i, 0))],
        out_shape=[jax.ShapeDtypeStruct((s, d), BF16), jax.ShapeDtypeStruct((s, d), BF16),
                   jax.ShapeDtypeStruct((s, 2 * d), BF16)],
        compiler_params=_params("parallel"))(dm, zr, zr, yp, yf)


def _xattn_probs(q, k):
    sc = lax.dot_general(q, k, _DIMS["nt"], preferred_element_type=F32) * (1.0 / math.sqrt(X_DH))
    p = jnp.exp(sc - jnp.max(sc, axis=-1, keepdims=True))
    return p / jnp.sum(p, axis=-1, keepdims=True)


def _xattn_fwd(q, kv, name):
    s = q.shape[0]
    m = kv.shape[0]
    tq = _pick(s, (512, 256, 128))

    def body(q_ref, k_ref, v_ref, o_ref):
        p = _xattn_probs(q_ref[...], k_ref[...])
        o_ref[...] = jnp.dot(p.astype(BF16), v_ref[...],
                             preferred_element_type=F32).astype(BF16)

    qspec = pl.BlockSpec((tq, X_DH), lambda i, hh: (i, hh))
    return pl.pallas_call(
        body, name=name, grid=(s // tq, X_HEADS),
        in_specs=[qspec, pl.BlockSpec((m, X_DH), lambda i, hh: (0, hh)),
                  pl.BlockSpec((m, X_DH), lambda i, hh: (0, X_HEADS + hh))],
        out_specs=qspec, out_shape=jax.ShapeDtypeStruct((s, X_W), BF16),
        compiler_params=_params("parallel", "parallel"))(q, kv, kv)


def _xattn_bwd(q, kv, do, name):
    s = q.shape[0]
    m = kv.shape[0]
    tq = _pick(s, (512, 256, 128))
    scale = 1.0 / math.sqrt(X_DH)

    def body(q_ref, k_ref, v_ref, do_ref, dq_ref, dk_ref, dv_ref):
        @pl.when(pl.program_id(1) == 0)
        def _():
            dk_ref[...] = jnp.zeros_like(dk_ref)
            dv_ref[...] = jnp.zeros_like(dv_ref)

        qv, kk, dov = q_ref[...], k_ref[...], do_ref[...]
        p = _xattn_probs(qv, kk)
        dv_ref[...] += lax.dot_general(p.astype(BF16), dov, _DIMS["tn"],
                                       preferred_element_type=F32)
        dp = lax.dot_general(dov, v_ref[...], _DIMS["nt"], preferred_element_type=F32)
        ds = p * (dp - jnp.sum(dp * p, axis=-1, keepdims=True))
        dsb = (ds * scale).astype(BF16)
        dq_ref[...] = jnp.dot(dsb, kk, preferred_element_type=F32).astype(BF16)
        dk_ref[...] += lax.dot_general(dsb, qv, _DIMS["tn"], preferred_element_type=F32)

    qspec = pl.BlockSpec((tq, X_DH), lambda hh, i: (i, hh))
    kspec = pl.BlockSpec((m, X_DH), lambda hh, i: (0, hh))
    return pl.pallas_call(
        body, name=name, grid=(X_HEADS, s // tq),
        in_specs=[qspec, kspec, pl.BlockSpec((m, X_DH), lambda hh, i: (0, X_HEADS + hh)), qspec],
        out_specs=[qspec, kspec, kspec],
        out_shape=[jax.ShapeDtypeStruct((s, X_W), BF16), jax.ShapeDtypeStruct((m, X_W), F32),
                   jax.ShapeDtypeStruct((m, X_W), F32)],
        compiler_params=_params("parallel", "arbitrary"))(q, kv, kv, do)


_GELU_C = math.sqrt(2.0 / math.pi)
CONV_HALO = 8


def _gelu(x):
    return 0.5 * x * (1.0 + jnp.tanh(_GELU_C * (x + 0.044715 * x * x * x)))


def _gelu_grad(x):
    th = jnp.tanh(_GELU_C * (x + 0.044715 * x * x * x))
    return 0.5 * (1.0 + th) + 0.5 * x * (1.0 - th * th) * _GELU_C * (1.0 + 3 * 0.044715 * x * x)


def _shift_down(z, halo, n):
    rows = lax.broadcasted_iota(jnp.int32, z.shape, 0)
    out = pltpu.roll(z, n, 0)
    for r in range(n):
        out = jnp.where(rows == r, halo[CONV_HALO - n + r:CONV_HALO - n + r + 1], out)
    return out


def _conv(z, halo, cw, cb):
    return cw[2:3] * z + cw[1:2] * _shift_down(z, halo, 1) + cw[0:1] * _shift_down(z, halo, 2) + cb


def _conv_specs(tm, tn, off):
    per = tm // CONV_HALO
    return [pl.BlockSpec((tm, tn), lambda j, i: (i, j + off)),
            pl.BlockSpec((CONV_HALO, tn), lambda j, i: (jnp.maximum(i * per - 1, 0), j + off)),
            pl.BlockSpec((3, tn), lambda j, i: (0, j + off)),
            pl.BlockSpec((1, tn), lambda j, i: (0, j + off))]


def _convglu_fwd(z, cw, cb, name):
    s = z.shape[0]
    tm = _pick(s, (512, 256, 128))
    tn = 256
    nj = D_FF // tn

    def body(zg_ref, hg_ref, cwg_ref, cbg_ref, zu_ref, hu_ref, cwu_ref, cbu_ref, a_ref):
        first = pl.program_id(1) == 0
        gc = _conv(zg_ref[...], jnp.where(first, 0.0, hg_ref[...]), cwg_ref[...], cbg_ref[...])
        uc = _conv(zu_ref[...], jnp.where(first, 0.0, hu_ref[...]), cwu_ref[...], cbu_ref[...])
        a_ref[...] = (_gelu(gc) * uc).astype(BF16)

    return pl.pallas_call(
        body, name=name, grid=(nj, s // tm),
        in_specs=_conv_specs(tm, tn, 0) + _conv_specs(tm, tn, nj),
        out_specs=pl.BlockSpec((tm, tn), lambda j, i: (i, j)),
        out_shape=jax.ShapeDtypeStruct((s, D_FF), BF16),
        compiler_params=_params("parallel", "parallel"))(z, z, cw, cb, z, z, cw, cb)


def _convglu_bwd_pre(z, da, cw, cb, name):
    s = z.shape[0]
    tm = _pick(s, (512, 256, 128))
    tn = 256
    nj = D_FF // tn

    def body(zg_ref, hg_ref, cwg_ref, cbg_ref, zu_ref, hu_ref, cwu_ref, cbu_ref, da_ref,
             dg_ref, du_ref, dcwg_ref, dcwu_ref, dcbg_ref, dcbu_ref):
        first = pl.program_id(1) == 0

        @pl.when(first)
        def _():
            for r in (dcwg_ref, dcwu_ref, dcbg_ref, dcbu_ref):
                r[...] = jnp.zeros_like(r)

        dav = da_ref[...].astype(F32)
        zg, zu = zg_ref[...], zu_ref[...]
        hg = jnp.where(first, 0.0, hg_ref[...])
        hu = jnp.where(first, 0.0, hu_ref[...])
        gc = _conv(zg, hg, cwg_ref[...], cbg_ref[...])
        uc = _conv(zu, hu, cwu_ref[...], cbu_ref[...])
        dgc = dav * uc * _gelu_grad(gc)
        duc = dav * _gelu(gc)
        dg_ref[...] = dgc
        du_ref[...] = duc
        for d, zz, hh, dcw_ref, dcb_ref in ((dgc, zg, hg, dcwg_ref, dcbg_ref),
                                            (duc, zu, hu, dcwu_ref, dcbu_ref)):
            dcw_ref[...] += jnp.concatenate(
                [jnp.sum(d * _shift_down(zz, hh, 2), axis=0, keepdims=True),
                 jnp.sum(d * _shift_down(zz, hh, 1), axis=0, keepdims=True),
                 jnp.sum(d * zz, axis=0, keepdims=True)], axis=0)
            dcb_ref[...] += jnp.sum(d, axis=0, keepdims=True)

    tile = pl.BlockSpec((tm, tn), lambda j, i: (i, j))
    wspec = pl.BlockSpec((3, tn), lambda j, i: (0, j))
    bspec = pl.BlockSpec((1, tn), lambda j, i: (0, j))
    return pl.pallas_call(
        body, name=name, grid=(nj, s // tm),
        in_specs=_conv_specs(tm, tn, 0) + _conv_specs(tm, tn, nj) + [tile],
        out_specs=[tile, tile, wspec, wspec, bspec, bspec],
        out_shape=[jax.ShapeDtypeStruct((s, D_FF), F32), jax.ShapeDtypeStruct((s, D_FF), F32),
                   jax.ShapeDtypeStruct((3, D_FF), F32), jax.ShapeDtypeStruct((3, D_FF), F32),
                   jax.ShapeDtypeStruct((1, D_FF), F32), jax.ShapeDtypeStruct((1, D_FF), F32)],
        compiler_params=_params("parallel", "arbitrary"))(z, z, cw, cb, z, z, cw, cb, da)


def _convglu_bwd_post(dzc, cw, off, name):
    s, n = dzc.shape
    tm = _pick(s, (512, 256, 128))
    tn = 256
    nt = s // tm
    per = tm // CONV_HALO

    def body(d_ref, n_ref, cw_ref, o_ref):
        d = d_ref[...]
        nxt = jnp.where(pl.program_id(1) == nt - 1, 0.0, n_ref[...])
        rows = lax.broadcasted_iota(jnp.int32, d.shape, 0)
        acc = cw_ref[2:3] * d
        for k in (1, 2):
            up = pltpu.roll(d, tm - k, 0)
            for r in range(k):
                up = jnp.where(rows == tm - k + r, nxt[r:r + 1], up)
            acc = acc + cw_ref[2 - k:3 - k] * up
        o_ref[...] = acc.astype(BF16)

    return pl.pallas_call(
        body, name=name, grid=(n // tn, nt),
        in_specs=[pl.BlockSpec((tm, tn), lambda j, i: (i, j)),
                  pl.BlockSpec((CONV_HALO, tn),
                               lambda j, i: (jnp.minimum((i + 1) * per, s // CONV_HALO - 1), j)),
                  pl.BlockSpec((3, tn), lambda j, i: (0, j + off))],
        out_specs=pl.BlockSpec((tm, tn), lambda j, i: (i, j)),
        out_shape=jax.ShapeDtypeStruct((s, n), BF16),
        compiler_params=_params("parallel", "parallel"))(dzc, dzc, cw)


ANY = pl.BlockSpec(memory_space=pl.ANY)


def _place():
    x, y, c = (lax.axis_index(a) for a in MESH_AXES)
    return x, y, c, [(1 - x, y), (x, 1 - y), (1 - x, 1 - y)]


def _allgather(src, name):
    r, cdim = src.shape

    def body(x_ref, out_ref, send_sems, recv_sems, local_sem):
        x, y, c, chips = _place()
        me, sibling = (x, y, c), (x, y, 1 - c)

        def row(px, py, pc):
            return out_ref.at[4 * px + 2 * py + pc]

        def copy(k, block, to, src_ref=None):
            return pltpu.make_async_remote_copy(
                src_ref=row(*block) if src_ref is None else src_ref, dst_ref=row(*block),
                send_sem=send_sems.at[k], recv_sem=recv_sems.at[k], device_id=to,
                device_id_type=MESH_ID)

        mine = pltpu.make_async_copy(x_ref, row(*me), local_sem)
        mine.start()
        first = [copy(0, me, sibling, src_ref=x_ref)]
        first += [copy(1 + j, me, (*chip, c), src_ref=x_ref) for j, chip in enumerate(chips)]
        for cp in first:
            cp.start()
        passed = [copy(4 + j, (*chip, c), sibling) for j, chip in enumerate(chips)]
        for j, chip in enumerate(chips):
            copy(1 + j, (*chip, c), me).wait_recv()
            passed[j].start()
        copy(0, sibling, me).wait_recv()
        for j, chip in enumerate(chips):
            copy(4 + j, (*chip, 1 - c), me).wait_recv()
        for cp in first + passed:
            cp.wait_send()
        mine.wait()

    return pl.pallas_call(
        body, name=name, in_specs=[ANY], out_specs=ANY,
        out_shape=jax.ShapeDtypeStruct((N_DEV, r, cdim), src.dtype),
        scratch_shapes=[pltpu.SemaphoreType.DMA((7,)), pltpu.SemaphoreType.DMA((7,)),
                        pltpu.SemaphoreType.DMA(())],
    )(src)


def _swap_with_sibling(buf, name):
    _, nchip, r, cdim = buf.shape

    def body(g_ref, rcv_ref, send_sems, recv_sems):
        x, y, c, _ = _place()
        copies = [pltpu.make_async_remote_copy(
            src_ref=g_ref.at[1 - c, k], dst_ref=rcv_ref.at[k], send_sem=send_sems.at[k],
            recv_sem=recv_sems.at[k], device_id=(x, y, 1 - c), device_id_type=MESH_ID)
            for k in range(nchip)]
        for cp in copies:
            cp.start()
        for cp in copies:
            cp.wait()

    return pl.pallas_call(
        body, name=name, in_specs=[ANY], out_specs=ANY,
        out_shape=jax.ShapeDtypeStruct((nchip, r, cdim), buf.dtype),
        scratch_shapes=[pltpu.SemaphoreType.DMA((nchip,)), pltpu.SemaphoreType.DMA((nchip,))],
    )(buf)


def _exchange_chips(buf, name):
    nchip, r, cdim = buf.shape

    def body(b_ref, rcv_ref, send_sems, recv_sems, local_sem):
        x, y, c, chips = _place()
        my_chip = 2 * x + y
        mine = pltpu.make_async_copy(b_ref.at[my_chip], rcv_ref.at[my_chip], local_sem)
        mine.start()
        copies = [pltpu.make_async_remote_copy(
            src_ref=b_ref.at[2 * px + py], dst_ref=rcv_ref.at[my_chip], send_sem=send_sems.at[j],
            recv_sem=recv_sems.at[j], device_id=(px, py, c), device_id_type=MESH_ID)
            for j, (px, py) in enumerate(chips)]
        for cp in copies:
            cp.start()
        for j, (px, py) in enumerate(chips):
            pltpu.make_async_remote_copy(
                src_ref=b_ref.at[my_chip], dst_ref=rcv_ref.at[2 * px + py],
                send_sem=send_sems.at[j], recv_sem=recv_sems.at[j], device_id=(px, py, c),
                device_id_type=MESH_ID).wait_recv()
        for cp in copies:
            cp.wait_send()
        mine.wait()

    return pl.pallas_call(
        body, name=name, in_specs=[ANY], out_specs=ANY,
        out_shape=jax.ShapeDtypeStruct((nchip, r, cdim), buf.dtype),
        scratch_shapes=[pltpu.SemaphoreType.DMA((3,)), pltpu.SemaphoreType.DMA((3,)),
                        pltpu.SemaphoreType.DMA(())],
    )(buf)


def _pair_add(buf, rcv, core, name):
    _, nchip, r, cdim = buf.shape
    tr = _pick(r, (FLAT_ROWS, 8))

    def body(c_ref, a_ref, b_ref, o_ref):
        o_ref[...] = (a_ref[...] + b_ref[...]).astype(BF16)

    grid_spec = pltpu.PrefetchScalarGridSpec(
        num_scalar_prefetch=1, grid=(nchip, r // tr),
        in_specs=[pl.BlockSpec((None, None, tr, cdim), lambda k, i, c_ref: (c_ref[0], k, i, 0)),
                  pl.BlockSpec((None, tr, cdim), lambda k, i, c_ref: (k, i, 0))],
        out_specs=pl.BlockSpec((None, tr, cdim), lambda k, i, c_ref: (k, i, 0)))
    return pl.pallas_call(
        body, name=name, grid_spec=grid_spec,
        out_shape=jax.ShapeDtypeStruct((nchip, r, cdim), BF16),
        compiler_params=_params("parallel", "parallel"))(core, buf, rcv)


def _adamw(parts, w, m, v, name):
    npart, r, cdim = parts.shape
    tr = _pick(r, (FLAT_ROWS, 64, 32, 8))
    c1 = 1.0 - ADAM_B1 ** ADAM_STEP
    c2 = 1.0 - ADAM_B2 ** ADAM_STEP

    def body(p_ref, w_ref, m_ref, v_ref, g_ref, d_ref, mo_ref, vo_ref):
        g = p_ref[0].astype(F32)
        for k in range(1, npart):
            g = g + p_ref[k].astype(F32)
        mn = ADAM_B1 * m_ref[...] + (1.0 - ADAM_B1) * g
        vn = ADAM_B2 * v_ref[...] + (1.0 - ADAM_B2) * (g * g)
        g_ref[...] = g
        mo_ref[...] = mn
        vo_ref[...] = vn
        d_ref[...] = -ADAM_LR * ((mn / c1) / (jnp.sqrt(vn / c2) + ADAM_EPS) + ADAM_WD * w_ref[...])

    row = pl.BlockSpec((tr, cdim), lambda i: (i, 0))
    return pl.pallas_call(
        body, name=name, grid=(r // tr,),
        in_specs=[pl.BlockSpec((npart, tr, cdim), lambda i: (0, i, 0)), row, row, row],
        out_specs=[row] * 4, out_shape=[jax.ShapeDtypeStruct((r, cdim), F32)] * 4,
        compiler_params=_params("parallel"))(parts, w, m, v)


def _piece_rows(shape):
    return shape[0] * _round_up(int(math.prod(shape[1:])), LANES) // LANES


def _to_rows(a, nlead):
    lead, depth = a.shape[:nlead], a.shape[nlead]
    per = int(math.prod(a.shape[nlead + 1:]))
    if per % LANES:
        a = jnp.pad(a.reshape(lead + (depth, per)),
                    [(0, 0)] * (nlead + 1) + [(0, _round_up(per, LANES) - per)])
    return a.reshape(lead + (-1, LANES))


def _pack(arrays, row_mult, nlead=0):
    rows = jnp.concatenate([_to_rows(a, nlead) for a in arrays], axis=nlead)
    pad = _round_up(rows.shape[nlead], row_mult) - rows.shape[nlead]
    return jnp.pad(rows, [(0, 0)] * nlead + [(0, pad), (0, 0)])


def _unpack(rows, shapes, nlead=0):
    lead = rows.shape[:nlead]
    out, r = [], 0
    for shp in shapes:
        n = _piece_rows(shp)
        piece = lax.slice_in_dim(rows, r, r + n, axis=nlead)
        per = int(math.prod(shp[1:]))
        if per % LANES:
            piece = piece.reshape(lead + (shp[0], -1))[..., :per]
        out.append(piece.reshape(lead + tuple(shp)))
        r += n
    return out


def _full_weight(piece, name, layer):
    blk = piece[:, layer]
    return jnp.concatenate([blk[dev] for dev in range(N_DEV)],
                           axis=1 if name in COL_SHARDED else 0)


def _by_destination(name, grads):
    g = jnp.stack(grads)
    if name in COL_SHARDED:
        cs = g.shape[2] // N_DEV
        return jnp.stack([g[:, :, dev * cs:(dev + 1) * cs] for dev in range(N_DEV)])
    rs = g.shape[1] // N_DEV
    return jnp.stack([g[:, dev * rs:(dev + 1) * rs] for dev in range(N_DEV)])


def _heads(a):
    s = a.shape[0]
    return a.reshape(s, FOX_HEADS, FOX_DH).transpose(1, 0, 2)


def _unheads(a):
    return a.transpose(1, 0, 2).reshape(a.shape[1], FOX_W)


def kernel(*args):
    p = dict(zip(INPUTS, args))
    x0 = p["x"][0]
    mem = p["mem"][0]
    tgt = p["loss_target"][0]
    s, d = x0.shape
    core = lax.axis_index("c").astype(jnp.int32).reshape(1)

    mm_names = tuple(n for n in SHARDED if n != "conv_w")
    shard_shapes = {n: p[n].shape for n in SHARDED}
    gathered = _allgather(_pack([p[n].astype(BF16) for n in mm_names], 8), "gather_weights")
    pieces = dict(zip(mm_names, _unpack(gathered, [shard_shapes[n] for n in mm_names], nlead=1)))
    conv_g = _allgather(_pack([p["conv_w"]], 8), "gather_conv_w")
    conv_piece = _unpack(conv_g, [shard_shapes["conv_w"]], nlead=1)[0]

    def weights_of(layer):
        w = {n: _full_weight(pieces[n], n, layer) for n in mm_names}
        w_in = w.pop("w_in")
        fpad = jnp.pad(w_in[:, OFF_F:OFF_GP], ((0, 0), (0, ZR_W - ZR_F - FOX_HEADS)))
        w["w_cat"] = jnp.concatenate(
            [w_in[:, OFF_Q:OFF_F], w_in[:, OFF_GP:OFF_GF], w_in[:, OFF_GF:IN_W],
             w_in[:, 0:OFF_Q], fpad], axis=1)
        w["conv_w"] = _full_weight(conv_piece, "conv_w", layer)
        w["conv_b"] = p["conv_b"][layer].reshape(1, -1)
        w["pool_w"] = p["pool_w"][layer].astype(BF16)
        w["pool_scale"] = p["pool_scale"][layer].reshape(1, -1)
        w["b128"] = jnp.pad(p["b_forget"][layer], (0, 128 - FOX_HEADS)).reshape(1, 128)
        return w

    saved = []
    _, h1 = _norm_fwd(x0, None, None, p["mix_pre_g"][0], "norm_first")
    x_in = x0
    y_final = None
    for l in range(DEPTH):
        w = weights_of(l)
        sv = {"w": w, "x0": x_in, "h1": h1}
        zqkv = _mm(h1, w["w_cat"][:, :QKV_W], "nn", BF16, "mm_qkv")
        zr = _mm(h1, w["w_cat"][:, QKV_W:], "nn", F32, "mm_zr")
        c = _forget_fwd(zr, w["b128"], "forget_fwd")
        pm = _pool_fwd(zr, w["pool_w"], w["pool_scale"], "pool_fwd")
        qh, kh, vh = (_heads(zqkv[:, k * FOX_W:(k + 1) * FOX_W]) for k in range(3))
        c_t = c[:, :FOX_HEADS].T
        c_col, c_row = c_t[:, :, None], c_t[:, None, :]
        oh, lse = _fox_fwd(qh, kh, vh, c_col, c_row, "fox_fwd")
        o = _unheads(oh).astype(BF16)
        yp = _mm(pm, w["w_pool_br"], "nn", F32, "mm_pool_br")
        yf = _mm(o, w["w_fox_br"], "nn", F32, "mm_fox_br")
        merged = _gate_fwd(zr, yp, yf, "gate_fwd")
        f1 = _mm(merged, w["w_mix_out"], "nn", F32, "mm_mix_out")
        x1, h2 = _norm_fwd(x_in, f1, p["mix_post_g"][l], p["xa_pre_g"][l], "norm_mix_xa")
        _, mem_n = _norm_fwd(mem, None, None, p["mem_g"][l], "norm_mem")
        q2 = _mm(h2, w["w_xq"], "nn", BF16, "mm_xq")
        kv = _mm(mem_n, w["w_xkv"], "nn", BF16, "mm_xkv")
        o2 = _xattn_fwd(q2, kv, "xattn_fwd")
        f2 = _mm(o2, w["w_xo"], "nn", F32, "mm_xo")
        x2, h3 = _norm_fwd(x1, f2, p["xa_post_g"][l], p["ffn_pre_g"][l], "norm_xa_ffn")
        z3 = _mm(h3, w["w_up"], "nn", F32, "mm_up")
        a = _convglu_fwd(z3, w["conv_w"], w["conv_b"], "convglu_fwd")
        f3 = _mm(a, w["w_down"], "nn", F32, "mm_down")
        if l + 1 < DEPTH:
            x3, h1 = _norm_fwd(x2, f3, p["ffn_post_g"][l], p["mix_pre_g"][l + 1], "norm_ffn_mix")
        else:
            x3, _ = _norm_fwd(x2, f3, p["ffn_post_g"][l], None, "norm_last")
            y_final = x3
        sv.update(zr=zr, qh=qh, kh=kh, vh=vh, c_col=c_col, c_row=c_row, oh=oh, lse=lse, o=o,
                  pm=pm, yp=yp, yf=yf, merged=merged, f1=f1, x1=x1, h2=h2, mem_n=mem_n, q2=q2,
                  kv=kv, o2=o2, f2=f2, x2=x2, h3=h3, z3=z3, a=a, f3=f3)
        saved.append(sv)
        x_in = x3

    top = _norm_bwd("loss_head", y=y_final, tgt=tgt, f_prev=saved[-1]["f3"],
                    g_post=p["ffn_post_g"][DEPTH - 1])
    loss = lax.psum(top["loss"][0, 0], MESH_AXES)
    dres, df3 = top["dx"], top["df"]
    gw = {n: [None] * DEPTH for n in WEIGHTS}
    gw["ffn_post_g"][DEPTH - 1] = top["dg_post"]
    for l in reversed(range(DEPTH)):
        sv = saved[l]
        w = sv["w"]
        gw["w_down"][l] = _mm(sv["a"], df3, "tn", F32, "mm_d_w_down")
        da = _mm(df3, w["w_down"], "nt", BF16, "mm_d_a")
        dgc, duc, dcwg, dcwu, dcbg, dcbu = _convglu_bwd_pre(sv["z3"], da, w["conv_w"],
                                                            w["conv_b"], "convglu_bwd_pre")
        gw["conv_w"][l] = jnp.concatenate([dcwg, dcwu], axis=1)
        gw["conv_b"][l] = jnp.concatenate([dcbg, dcbu], axis=1)
        dz3 = jnp.concatenate(
            [_convglu_bwd_post(dgc, w["conv_w"], 0, "convglu_bwd_post_g"),
             _convglu_bwd_post(duc, w["conv_w"], D_FF // 256, "convglu_bwd_post_u")], axis=1)
        gw["w_up"][l] = _mm(sv["h3"], dz3, "tn", F32, "mm_d_w_up")
        dh3 = _mm(dz3, w["w_up"], "nt", F32, "mm_d_h3")
        nb = _norm_bwd("norm_bwd_ffn_xa", dh=dh3, x=sv["x2"], g_pre=p["ffn_pre_g"][l], dres=dres,
                       f_prev=sv["f2"], g_post=p["xa_post_g"][l])
        gw["ffn_pre_g"][l], gw["xa_post_g"][l] = nb["dg_pre"], nb["dg_post"]
        dres, df2 = nb["dx"], nb["df"]
        gw["w_xo"][l] = _mm(sv["o2"], df2, "tn", F32, "mm_d_w_xo")
        do2 = _mm(df2, w["w_xo"], "nt", BF16, "mm_d_o2")
        dq2, dk2, dv2 = _xattn_bwd(sv["q2"], sv["kv"], do2, "xattn_bwd")
        dkv = jnp.concatenate([dk2, dv2], axis=1)
        gw["w_xq"][l] = _mm(sv["h2"], dq2, "tn", F32, "mm_d_w_xq")
        dh2 = _mm(dq2, w["w_xq"], "nt", F32, "mm_d_h2")
        gw["w_xkv"][l] = _mm(sv["mem_n"], dkv, "tn", F32, "mm_d_w_xkv")
        dmem_n = _mm(dkv, w["w_xkv"], "nt", F32, "mm_d_mem")
        gw["mem_g"][l] = _rms_dg(mem, p["mem_g"][l], dmem_n, "norm_mem_bwd")
        nb = _norm_bwd("norm_bwd_xa_mix", dh=dh2, x=sv["x1"], g_pre=p["xa_pre_g"][l], dres=dres,
                       f_prev=sv["f1"], g_post=p["mix_post_g"][l])
        gw["xa_pre_g"][l], gw["mix_post_g"][l] = nb["dg_pre"], nb["dg_post"]
        dres, df1 = nb["dx"], nb["df"]
        gw["w_mix_out"][l] = _mm(sv["merged"], df1, "tn", F32, "mm_d_w_mix_out")
        dmerged = _mm(df1, w["w_mix_out"], "nt", F32, "mm_d_merged")
        dyp, dyf, dzg = _gate_bwd(dmerged, sv["zr"], sv["yp"], sv["yf"], "gate_bwd")
        gw["w_pool_br"][l] = _mm(sv["pm"], dyp, "tn", F32, "mm_d_w_pool_br")
        dpm = _mm(dyp, w["w_pool_br"], "nt", F32, "mm_d_pm")
        gw["w_fox_br"][l] = _mm(sv["o"], dyf, "tn", F32, "mm_d_w_fox_br")
        do = _mm(dyf, w["w_fox_br"], "nt", BF16, "mm_d_o")
        du, dpw, dsc = _pool_bwd(sv["zr"], dpm, w["pool_w"], w["pool_scale"], "pool_bwd")
        gw["pool_w"][l], gw["pool_scale"][l] = dpw, dsc
        dqh, dkh, dvh, dc = _fox_bwd(sv["qh"], sv["kh"], sv["vh"], sv["c_col"], sv["c_row"],
                                     sv["oh"], sv["lse"], _heads(do), "fox_bwd")
        dc_rows = jnp.pad(dc[:, 0, :].T, ((0, 0), (0, 128 - FOX_HEADS)))
        dzf, db = _forget_bwd(dc_rows, sv["zr"], w["b128"], "forget_bwd")
        gw["b_forget"][l] = db[0, :FOX_HEADS]
        dzf_pad = jnp.pad(dzf[:, :FOX_HEADS], ((0, 0), (0, ZR_W - ZR_F - FOX_HEADS)))
        dzc = jnp.concatenate([_unheads(dqh).astype(BF16), _unheads(dkh).astype(BF16),
                               _unheads(dvh).astype(BF16), dzg, du, dzf_pad.astype(BF16)], axis=1)
        dw_cat = _mm(sv["h1"], dzc, "tn", F32, "mm_d_w_in")
        gw["w_in"][l] = jnp.concatenate(
            [dw_cat[:, QKV_W + ZR_U:QKV_W + ZR_F], dw_cat[:, :QKV_W],
             dw_cat[:, QKV_W + ZR_F:QKV_W + ZR_F + FOX_HEADS],
             dw_cat[:, QKV_W + ZR_GP:QKV_W + ZR_U]], axis=1)
        dh1 = _mm(dzc, w["w_cat"], "nt", F32, "mm_d_h1")
        if l > 0:
            nb = _norm_bwd("norm_bwd_mix_ffn", dh=dh1, x=sv["x0"], g_pre=p["mix_pre_g"][l],
                           dres=dres, f_prev=saved[l - 1]["f3"], g_post=p["ffn_post_g"][l - 1])
            gw["ffn_post_g"][l - 1] = nb["dg_post"]
            df3 = nb["df"]
        else:
            nb = _norm_bwd("norm_bwd_first", dh=dh1, x=sv["x0"], g_pre=p["mix_pre_g"][l], dres=dres)
        gw["mix_pre_g"][l] = nb["dg_pre"]
        dres = nb["dx"]
    grad_x = dres[None]

    by_dest = _pack([_by_destination(n, gw[n]) for n in SHARDED], FLAT_ROWS, nlead=1)
    buf = by_dest.reshape(4, 2, -1, LANES).transpose(1, 0, 2, 3)
    rcv = _swap_with_sibling(buf, "grads_to_sibling")
    pair = _pair_add(buf, rcv, core, "grads_pair_add")
    parts = _exchange_chips(pair, "grads_to_chips")
    shapes = [shard_shapes[n] for n in SHARDED]
    packed = [_pack([p[pre + n] for n in SHARDED], FLAT_ROWS) for pre in ("", "m_", "v_")]
    res_sh = [dict(zip(SHARDED, _unpack(r, shapes)))
              for r in _adamw(parts, *packed, "adamw_sharded")]

    rep_shapes = [p[n].shape for n in REPLICATED]
    part = _pack([jnp.stack(gw[n]).reshape(p[n].shape) for n in REPLICATED], 8)
    allparts = _allgather(part, "gather_small_grads")
    packed = [_pack([p[pre + n] for n in REPLICATED], 8) for pre in ("", "m_", "v_")]
    res_rep = [dict(zip(REPLICATED, _unpack(r, rep_shapes)))
               for r in _adamw(allparts, *packed, "adamw_replicated")]

    outs = [loss, grad_x]
    for k in range(4):
        outs += [res_sh[k][n] if n in SHARDED else res_rep[k][n] for n in WEIGHTS]
    return tuple(outs)
```

```python
import functools
import math

import jax
import jax.numpy as jnp
from jax import lax
from jax.experimental import pallas as pl
from jax.experimental.pallas import tpu as pltpu

F32 = jnp.float32
BF16 = jnp.bfloat16
MESH_AXES = ("x", "y", "c")
N_DEV = 8
MESH_ID = pl.DeviceIdType.MESH

DEPTH = 4
POOL_WINDOWS = (2, 4, 8, 16)
POOL_GROUP = 128
POOL_W = 512
POOL_HALO = 16
FOX_HEADS = 8
FOX_DH = 64
FOX_W = 512
X_HEADS = 4
X_DH = 128
X_W = 512
D_FF = 2816
RMS_EPS = 1e-6
ADAM_LR, ADAM_B1, ADAM_B2, ADAM_EPS, ADAM_WD, ADAM_STEP = 0.001, 0.9, 0.999, 1e-08, 0.01, 10

OFF_Q, OFF_F, OFF_GP, OFF_GF, IN_W = 512, 2048, 2056, 3080, 4104
ZR_GP, ZR_GF, ZR_U, ZR_F, ZR_W = 0, 1024, 2048, 2560, 3072
QKV_W = 3 * FOX_W
ZC_W = QKV_W + ZR_W

LANES = 1024
FLAT_ROWS = 128

SHARDED = ("w_in", "w_pool_br", "w_fox_br", "w_mix_out", "w_xq", "w_xkv", "w_xo", "w_up",
           "conv_w", "w_down")
COL_SHARDED = ("w_in", "w_pool_br", "w_fox_br", "w_xo", "w_up", "conv_w")
REPLICATED = ("mix_pre_g", "mix_post_g", "b_forget", "pool_w", "pool_scale", "xa_pre_g",
              "xa_post_g", "mem_g", "ffn_pre_g", "ffn_post_g", "conv_b")
WEIGHTS = ("mix_pre_g", "mix_post_g", "w_in", "b_forget", "pool_w", "pool_scale", "w_pool_br",
           "w_fox_br", "w_mix_out", "xa_pre_g", "xa_post_g", "mem_g", "w_xq", "w_xkv", "w_xo",
           "ffn_pre_g", "ffn_post_g", "w_up", "conv_w", "conv_b", "w_down")
INPUTS = (("x", "mem") + WEIGHTS + ("loss_target",) + tuple("m_" + n for n in WEIGHTS)
          + tuple("v_" + n for n in WEIGHTS))


def _pick(n, prefs):
    for p in prefs:
        if n % p == 0:
            return p
    return n


def _ktile(k):
    if k <= 2816:
        return k
    return _pick(k, (2816, 2048, 1536, 1024, 512))


def _round_up(n, m):
    return (n + m - 1) // m * m


def _params(*sem):
    return pltpu.CompilerParams(dimension_semantics=sem)


_DIMS = {"nn": (((1,), (0,)), ((), ())), "nt": (((1,), (1,)), ((), ())),
         "tn": (((0,), (0,)), ((), ()))}


def _mm(a, b, mode, out_dtype, name):
    if mode == "nn":
        (m, k), (_, n) = a.shape, b.shape
    elif mode == "nt":
        (m, k), (n, _) = a.shape, b.shape
    else:
        (k, m), (_, n) = a.shape, b.shape
    tm = _pick(m, (1024, 512, 256, 128))
    tn = _pick(n, (512, 256, 128))
    tk = _ktile(k)
    nk = k // tk

    def body(a_ref, b_ref, o_ref, acc_ref):
        kk = pl.program_id(2)

        @pl.when(kk == 0)
        def _():
            acc_ref[...] = jnp.zeros_like(acc_ref)

        acc_ref[...] += lax.dot_general(a_ref[...].astype(BF16), b_ref[...].astype(BF16),
                                        _DIMS[mode], preferred_element_type=F32)

        @pl.when(kk == nk - 1)
        def _():
            o_ref[...] = acc_ref[...].astype(out_dtype)

    if mode == "tn":
        a_spec = pl.BlockSpec((tk, tm), lambda i, j, kk: (kk, i))
    else:
        a_spec = pl.BlockSpec((tm, tk), lambda i, j, kk: (i, kk))
    if mode == "nt":
        b_spec = pl.BlockSpec((tn, tk), lambda i, j, kk: (j, kk))
    else:
        b_spec = pl.BlockSpec((tk, tn), lambda i, j, kk: (kk, j))
    return pl.pallas_call(
        body, name=name, grid=(m // tm, n // tn, nk),
        in_specs=[a_spec, b_spec],
        out_specs=pl.BlockSpec((tm, tn), lambda i, j, kk: (i, j)),
        out_shape=jax.ShapeDtypeStruct((m, n), out_dtype),
        scratch_shapes=[pltpu.VMEM((tm, tn), F32)],
        compiler_params=_params("parallel", "parallel", "arbitrary"),
    )(a, b)


def _rms(x, g):
    r = lax.rsqrt(jnp.mean(x * x, axis=-1, keepdims=True) + RMS_EPS)
    return x * r * g


def _rms_bwd(x, g, dy):
    r = lax.rsqrt(jnp.mean(x * x, axis=-1, keepdims=True) + RMS_EPS)
    xh = x * r
    t = dy * g
    dx = r * (t - xh * jnp.mean(t * xh, axis=-1, keepdims=True))
    dg = jnp.sum(dy * xh, axis=0, keepdims=True)
    return dx, dg


def _norm_fwd(x_in, f, g_post, g_pre, name):
    s, d = x_in.shape
    tm = _pick(s, (512, 256, 128))
    has_post, has_pre = f is not None, g_pre is not None
    row = pl.BlockSpec((tm, d), lambda i: (i, 0))
    vec = pl.BlockSpec((1, d), lambda i: (0, 0))

    def body(*refs):
        refs = list(refs)
        x = refs.pop(0)[...]
        if has_post:
            fv = refs.pop(0)[...]
            x = x + _rms(fv, refs.pop(0)[...])
        gpre = refs.pop(0)[...] if has_pre else None
        if has_post:
            refs.pop(0)[...] = x
        if has_pre:
            refs.pop(0)[...] = _rms(x, gpre).astype(BF16)

    ins, specs, outs, ospecs = [x_in], [row], [], []
    if has_post:
        ins += [f, g_post.reshape(1, d)]
        specs += [row, vec]
        outs.append(jax.ShapeDtypeStruct((s, d), F32))
        ospecs.append(row)
    if has_pre:
        ins.append(g_pre.reshape(1, d))
        specs.append(vec)
        outs.append(jax.ShapeDtypeStruct((s, d), BF16))
        ospecs.append(row)
    res = pl.pallas_call(body, name=name, grid=(s // tm,), in_specs=specs, out_specs=ospecs,
                         out_shape=outs, compiler_params=_params("parallel"))(*ins)
    res = list(res)
    x_out = res.pop(0) if has_post else None
    h = res.pop(0) if has_pre else None
    return x_out, h


def _norm_bwd(name, *, dh=None, x=None, g_pre=None, dres=None, y=None, tgt=None, f_prev=None,
              g_post=None):
    top = y is not None
    has_post = f_prev is not None
    ref_arr = y if top else x
    s, d = ref_arr.shape
    tm = _pick(s, (512, 256, 128))
    row = pl.BlockSpec((tm, d), lambda i: (i, 0))
    vec = pl.BlockSpec((1, d), lambda i: (0, 0))
    one = pl.BlockSpec((1, 1), lambda i: (0, 0))

    def body(*refs):
        refs = list(refs)
        i = pl.program_id(0)
        if top:
            yv, tv = refs.pop(0)[...], refs.pop(0)[...]
        else:
            dhv, xv, gv, dr = (refs.pop(0)[...].astype(F32), refs.pop(0)[...], refs.pop(0)[...],
                               refs.pop(0)[...])
        if has_post:
            fv, gp = refs.pop(0)[...], refs.pop(0)[...]
        dx_ref = refs.pop(0)
        if top:
            loss_ref = refs.pop(0)
        else:
            dgpre_ref = refs.pop(0)
        if has_post:
            df_ref, dgpost_ref = refs.pop(0), refs.pop(0)

        if top:
            err = yv - tv
            dx = err * (1.0 / d)
            part = 0.5 * jnp.sum(jnp.sum(err * err, axis=-1, keepdims=True) * (1.0 / d),
                                 axis=0, keepdims=True)
        else:
            dxn, dgpre = _rms_bwd(xv, gv, dhv)
            dx = dr + dxn
        dx_ref[...] = dx
        if has_post:
            df, dgpost = _rms_bwd(fv, gp, dx)
            df_ref[...] = df.astype(BF16)

        @pl.when(i == 0)
        def _():
            if top:
                loss_ref[...] = jnp.zeros_like(loss_ref)
            else:
                dgpre_ref[...] = jnp.zeros_like(dgpre_ref)
            if has_post:
                dgpost_ref[...] = jnp.zeros_like(dgpost_ref)

        if top:
            loss_ref[...] += part
        else:
            dgpre_ref[...] += dgpre
        if has_post:
            dgpost_ref[...] += dgpost

    if top:
        ins, specs = [y, tgt], [row, row]
    else:
        ins, specs = [dh, x, g_pre.reshape(1, d), dres], [row, row, vec, row]
    if has_post:
        ins += [f_prev, g_post.reshape(1, d)]
        specs += [row, vec]
    outs, ospecs = [jax.ShapeDtypeStruct((s, d), F32)], [row]
    if top:
        outs.append(jax.ShapeDtypeStruct((1, 1), F32))
        ospecs.append(one)
    else:
        outs.append(jax.ShapeDtypeStruct((1, d), F32))
        ospecs.append(vec)
    if has_post:
        outs += [jax.ShapeDtypeStruct((s, d), BF16), jax.ShapeDtypeStruct((1, d), F32)]
        ospecs += [row, vec]
    res = list(pl.pallas_call(body, name=name, grid=(s // tm,), in_specs=specs,
                              out_specs=ospecs, out_shape=outs,
                              compiler_params=_params("arbitrary"))(*ins))
    out = {"dx": res.pop(0)}
    out["loss" if top else "dg_pre"] = res.pop(0)
    if has_post:
        out["df"], out["dg_post"] = res.pop(0), res.pop(0)
    return out


def _rms_dg(x, g, dy, name):
    s, d = x.shape
    tm = _pick(s, (256, 128))
    row = pl.BlockSpec((tm, d), lambda i: (i, 0))
    vec = pl.BlockSpec((1, d), lambda i: (0, 0))

    def body(x_ref, g_ref, dy_ref, dg_ref):
        @pl.when(pl.program_id(0) == 0)
        def _():
            dg_ref[...] = jnp.zeros_like(dg_ref)

        dg_ref[...] += _rms_bwd(x_ref[...], g_ref[...], dy_ref[...])[1]

    return pl.pallas_call(body, name=name, grid=(s // tm,), in_specs=[row, vec, row],
                          out_specs=vec, out_shape=jax.ShapeDtypeStruct((1, d), F32),
                          compiler_params=_params("arbitrary"))(x, g.reshape(1, d), dy)


def _forget_fwd(zr, b128, name):
    s = zr.shape[0]
    tm = _pick(s, (256, 128))
    fblk = ZR_F // 128

    def body(z_ref, b_ref, c_ref, carry):
        @pl.when(pl.program_id(0) == 0)
        def _():
            carry[...] = jnp.zeros_like(carry)

        a = z_ref[...] + b_ref[...]
        acc = jnp.minimum(a, 0.0) - jnp.log1p(jnp.exp(-jnp.abs(a)))
        rows = lax.broadcasted_iota(jnp.int32, acc.shape, 0)
        k = 1
        while k < tm:
            acc = acc + jnp.where(rows >= k, pltpu.roll(acc, k, 0), 0.0)
            k *= 2
        acc = acc + carry[...]
        c_ref[...] = acc
        carry[...] = acc[tm - 1:tm, :]

    return pl.pallas_call(
        body, name=name, grid=(s // tm,),
        in_specs=[pl.BlockSpec((tm, 128), lambda i: (i, fblk)),
                  pl.BlockSpec((1, 128), lambda i: (0, 0))],
        out_specs=pl.BlockSpec((tm, 128), lambda i: (i, 0)),
        out_shape=jax.ShapeDtypeStruct((s, 128), F32),
        scratch_shapes=[pltpu.VMEM((1, 128), F32)],
        compiler_params=_params("arbitrary"))(zr, b128)


def _forget_bwd(dc, zr, b128, name):
    s = zr.shape[0]
    tm = _pick(s, (256, 128))
    nt = s // tm
    fblk = ZR_F // 128

    def body(dc_ref, z_ref, b_ref, dz_ref, db_ref, carry):
        @pl.when(pl.program_id(0) == 0)
        def _():
            carry[...] = jnp.zeros_like(carry)
            db_ref[...] = jnp.zeros_like(db_ref)

        acc = dc_ref[...]
        rows = lax.broadcasted_iota(jnp.int32, acc.shape, 0)
        k = 1
        while k < tm:
            acc = acc + jnp.where(rows < tm - k, pltpu.roll(acc, tm - k, 0), 0.0)
            k *= 2
        acc = acc + carry[...]
        carry[...] = acc[0:1, :]
        a = z_ref[...] + b_ref[...]
        dz = acc / (1.0 + jnp.exp(a))
        dz_ref[...] = dz
        db_ref[...] += jnp.sum(dz, axis=0, keepdims=True)

    return pl.pallas_call(
        body, name=name, grid=(nt,),
        in_specs=[pl.BlockSpec((tm, 128), lambda i: (nt - 1 - i, 0)),
                  pl.BlockSpec((tm, 128), lambda i: (nt - 1 - i, fblk)),
                  pl.BlockSpec((1, 128), lambda i: (0, 0))],
        out_specs=[pl.BlockSpec((tm, 128), lambda i: (nt - 1 - i, 0)),
                   pl.BlockSpec((1, 128), lambda i: (0, 0))],
        out_shape=[jax.ShapeDtypeStruct((s, 128), F32), jax.ShapeDtypeStruct((1, 128), F32)],
        scratch_shapes=[pltpu.VMEM((1, 128), F32)],
        compiler_params=_params("arbitrary"))(dc, zr, b128)


def _pooled(ext, t_abs, g, w):
    e = ext[:, g * POOL_GROUP:(g + 1) * POOL_GROUP]
    acc = e
    k = 1
    while k < w:
        acc = acc + pltpu.roll(acc, k, 0)
        k *= 2
    cnt = jnp.minimum(t_abs + 1, w).astype(F32)
    return acc[POOL_HALO:] / cnt - e[POOL_HALO:]


def _pool_specs(s, tm):
    per = tm // POOL_HALO
    ublk = ZR_U // POOL_W
    return [pl.BlockSpec((tm, POOL_W), lambda i: (i, ublk)),
            pl.BlockSpec((POOL_HALO, POOL_W), lambda i: (jnp.maximum(i * per - 1, 0), ublk))]


def _pool_fwd(zr, pw, scale, name):
    s = zr.shape[0]
    tm = _pick(s, (512, 256, 128))

    def body(u_ref, h_ref, pw_ref, sc_ref, o_ref):
        i = pl.program_id(0)
        halo = jnp.where(i > 0, h_ref[...], 0.0)
        ext = jnp.concatenate([halo, u_ref[...]], axis=0)
        t_abs = i * tm + lax.broadcasted_iota(jnp.int32, (tm, 1), 0)
        outs = []
        for g, w in enumerate(POOL_WINDOWS):
            pooled = _pooled(ext, t_abs, g, w)
            outs.append(jnp.dot(pooled.astype(BF16), pw_ref[g], preferred_element_type=F32))
        o_ref[...] = (jnp.concatenate(outs, axis=1) * sc_ref[...]).astype(BF16)

    return pl.pallas_call(
        body, name=name, grid=(s // tm,),
        in_specs=_pool_specs(s, tm) + [
            pl.BlockSpec((len(POOL_WINDOWS), POOL_GROUP, POOL_GROUP), lambda i: (0, 0, 0)),
            pl.BlockSpec((1, POOL_W), lambda i: (0, 0))],
        out_specs=pl.BlockSpec((tm, POOL_W), lambda i: (i, 0)),
        out_shape=jax.ShapeDtypeStruct((s, POOL_W), BF16),
        compiler_params=_params("parallel"))(zr, zr, pw, scale)


def _pool_bwd(zr, dpm, pw, scale, name):
    s = zr.shape[0]
    tm = _pick(s, (512, 256, 128))
    nt = s // tm
    per = tm // POOL_HALO
    n_ext = tm + POOL_HALO
    ng = len(POOL_WINDOWS)

    def body(u_ref, h_ref, d_ref, dn_ref, pw_ref, sc_ref, du_ref, dpw_ref, dsc_ref):
        i = pl.program_id(0)

        @pl.when(i == 0)
        def _():
            dpw_ref[...] = jnp.zeros_like(dpw_ref)
            dsc_ref[...] = jnp.zeros_like(dsc_ref)

        halo = jnp.where(i > 0, h_ref[...], 0.0)
        ext_u = jnp.concatenate([halo, u_ref[...]], axis=0)
        nxt = jnp.where(i < nt - 1, dn_ref[...], 0.0)
        ext_d = jnp.concatenate([d_ref[...], nxt], axis=0)
        t_abs = i * tm + lax.broadcasted_iota(jnp.int32, (tm, 1), 0)
        t_ext = i * tm + lax.broadcasted_iota(jnp.int32, (n_ext, 1), 0)
        dus, dscs = [], []
        for g, w in enumerate(POOL_WINDOWS):
            sl = slice(g * POOL_GROUP, (g + 1) * POOL_GROUP)
            pooled = _pooled(ext_u, t_abs, g, w).astype(BF16)
            mixed = jnp.dot(pooled, pw_ref[g], preferred_element_type=F32)
            d_g = ext_d[:, sl]
            dscs.append(jnp.sum(d_g[:tm] * mixed, axis=0, keepdims=True))
            dmixed = (d_g * sc_ref[:, sl]).astype(BF16)
            dpw_ref[g] += lax.dot_general(pooled, dmixed[:tm], _DIMS["tn"],
                                          preferred_element_type=F32)
            dpooled = lax.dot_general(dmixed, pw_ref[g], _DIMS["nt"], preferred_element_type=F32)
            acc = dpooled / jnp.minimum(t_ext + 1, w).astype(F32)
            k = 1
            while k < w:
                acc = acc + pltpu.roll(acc, n_ext - k, 0)
                k *= 2
            dus.append(acc[:tm] - dpooled[:tm])
        du_ref[...] = jnp.concatenate(dus, axis=1).astype(BF16)
        dsc_ref[...] += jnp.concatenate(dscs, axis=1)

    return pl.pallas_call(
        body, name=name, grid=(nt,),
        in_specs=_pool_specs(s, tm) + [
            pl.BlockSpec((tm, POOL_W), lambda i: (i, 0)),
            pl.BlockSpec((POOL_HALO, POOL_W),
                         lambda i: (jnp.minimum((i + 1) * per, s // POOL_HALO - 1), 0)),
            pl.BlockSpec((ng, POOL_GROUP, POOL_GROUP), lambda i: (0, 0, 0)),
            pl.BlockSpec((1, POOL_W), lambda i: (0, 0))],
        out_specs=[pl.BlockSpec((tm, POOL_W), lambda i: (i, 0)),
                   pl.BlockSpec((ng, POOL_GROUP, POOL_GROUP), lambda i: (0, 0, 0)),
                   pl.BlockSpec((1, POOL_W), lambda i: (0, 0))],
        out_shape=[jax.ShapeDtypeStruct((s, POOL_W), BF16),
                   jax.ShapeDtypeStruct((ng, POOL_GROUP, POOL_GROUP), F32),
                   jax.ShapeDtypeStruct((1, POOL_W), F32)],
        compiler_params=_params("arbitrary"))(zr, zr, dpm, dpm, pw, scale)


def _fox_logits(q, k, cq, ck, scale, diagonal):
    sc = lax.dot_general(q * scale, k, _DIMS["nt"], preferred_element_type=F32) + cq - ck
    if diagonal:
        rows = lax.broadcasted_iota(jnp.int32, sc.shape, 0)
        cols = lax.broadcasted_iota(jnp.int32, sc.shape, 1)
        sc = jnp.where(cols <= rows, sc, -jnp.inf)
    return sc


def _fox_fwd(q, k, v, c_col, c_row, name):
    h, s, dh = q.shape
    t = _pick(s, (512, 256, 128))
    nb = s // t
    scale = 1.0 / math.sqrt(dh)

    pairs = [(i, j) for i in range(nb) for j in range(i + 1)]
    qi_tab = jnp.asarray([pr[0] for pr in pairs], jnp.int32)
    kj_tab = jnp.asarray([pr[1] for pr in pairs], jnp.int32)

    def body(qi_ref, kj_ref, q_ref, k_ref, v_ref, cq_ref, ck_ref, o_ref, lse_ref, m_sc, l_sc,
             acc_sc):
        i, j = qi_ref[pl.program_id(1)], kj_ref[pl.program_id(1)]

        @pl.when(j == 0)
        def _():
            m_sc[...] = jnp.full_like(m_sc, -jnp.inf)
            l_sc[...] = jnp.zeros_like(l_sc)
            acc_sc[...] = jnp.zeros_like(acc_sc)

        def step(diagonal):
            sc = _fox_logits(q_ref[...], k_ref[...], cq_ref[...], ck_ref[...], scale, diagonal)
            m_new = jnp.maximum(m_sc[...], jnp.max(sc, axis=-1, keepdims=True))
            alpha = jnp.exp(m_sc[...] - m_new)
            p = jnp.exp(sc - m_new)
            l_sc[...] = alpha * l_sc[...] + jnp.sum(p, axis=-1, keepdims=True)
            p_hi = p.astype(BF16)
            p_lo = (p - p_hi.astype(F32)).astype(BF16)
            vv = v_ref[...]
            acc_sc[...] = (alpha * acc_sc[...] + jnp.dot(p_hi, vv, preferred_element_type=F32)
                           + jnp.dot(p_lo, vv, preferred_element_type=F32))
            m_sc[...] = m_new

        pl.when(j < i)(functools.partial(step, False))
        pl.when(j == i)(functools.partial(step, True))

        @pl.when(j == i)
        def _():
            o_ref[...] = acc_sc[...] / l_sc[...]
            lse_ref[...] = m_sc[...] + jnp.log(l_sc[...])

    qspec = pl.BlockSpec((None, t, dh), lambda hh, n, qi, kj: (hh, qi[n], 0))
    kspec = pl.BlockSpec((None, t, dh), lambda hh, n, qi, kj: (hh, kj[n], 0))
    colspec = pl.BlockSpec((None, t, 1), lambda hh, n, qi, kj: (hh, qi[n], 0))
    grid_spec = pltpu.PrefetchScalarGridSpec(
        num_scalar_prefetch=2, grid=(h, len(pairs)),
        in_specs=[qspec, kspec, kspec, colspec,
                  pl.BlockSpec((None, 1, t), lambda hh, n, qi, kj: (hh, 0, kj[n]))],
        out_specs=[qspec, colspec],
        scratch_shapes=[pltpu.VMEM((t, 1), F32), pltpu.VMEM((t, 1), F32),
                        pltpu.VMEM((t, dh), F32)])
    return pl.pallas_call(
        body, name=name, grid_spec=grid_spec,
        out_shape=[jax.ShapeDtypeStruct((h, s, dh), F32), jax.ShapeDtypeStruct((h, s, 1), F32)],
        compiler_params=_params("parallel", "arbitrary"))(qi_tab, kj_tab, q, k, v, c_col, c_row)


def _fox_bwd(q, k, v, c_col, c_row, o, lse, do, name):
    h, s, dh = q.shape
    t = _pick(s, (512, 256, 128))
    nb = s // t
    scale = 1.0 / math.sqrt(dh)

    pairs = [(i, j) for j in range(nb) for i in range(j, nb)]
    qi_tab = jnp.asarray([pr[0] for pr in pairs], jnp.int32)
    kj_tab = jnp.asarray([pr[1] for pr in pairs], jnp.int32)

    def body(qi_ref, kj_ref, q_ref, k_ref, v_ref, cq_ref, ck_ref, o_ref, lse_ref, do_ref,
             dq_ref, dk_ref, dv_ref, dc_ref, dk_sc, dv_sc, dc_sc):
        i, j = qi_ref[pl.program_id(1)], kj_ref[pl.program_id(1)]

        @pl.when(pl.program_id(1) == 0)
        def _():
            dq_ref[...] = jnp.zeros_like(dq_ref)

        @pl.when(i == j)
        def _():
            dk_sc[...] = jnp.zeros_like(dk_sc)
            dv_sc[...] = jnp.zeros_like(dv_sc)
            dc_sc[...] = jnp.zeros_like(dc_sc)

        def step(diagonal):
            qv, kv, dov = q_ref[...], k_ref[...], do_ref[...]
            sc = _fox_logits(qv, kv, cq_ref[...], ck_ref[...], scale, diagonal)
            p = jnp.exp(sc - lse_ref[...])
            delta = jnp.sum(dov.astype(F32) * o_ref[...], axis=-1, keepdims=True)
            dv_sc[...] += lax.dot_general(p.astype(BF16), dov, _DIMS["tn"],
                                          preferred_element_type=F32)
            dp = lax.dot_general(dov, v_ref[...], _DIMS["nt"], preferred_element_type=F32)
            ds = p * (dp - delta)
            dc_sc[...] += jnp.sum(ds, axis=0, keepdims=True)
            dsb = ds.astype(BF16)
            dk_sc[...] += scale * lax.dot_general(dsb, qv, _DIMS["tn"],
                                                  preferred_element_type=F32)
            rows = pl.ds(pl.multiple_of(i * t, t), t)
            dq_ref[rows, :] += scale * jnp.dot(dsb, kv, preferred_element_type=F32)

        pl.when(i > j)(functools.partial(step, False))
        pl.when(i == j)(functools.partial(step, True))

        @pl.when(i == nb - 1)
        def _():
            dk_ref[...] = dk_sc[...]
            dv_ref[...] = dv_sc[...]
            dc_ref[...] = -dc_sc[...]

    qspec = pl.BlockSpec((None, t, dh), lambda hh, n, qi, kj: (hh, qi[n], 0))
    kspec = pl.BlockSpec((None, t, dh), lambda hh, n, qi, kj: (hh, kj[n], 0))
    qcol = pl.BlockSpec((None, t, 1), lambda hh, n, qi, kj: (hh, qi[n], 0))
    krow = pl.BlockSpec((None, 1, t), lambda hh, n, qi, kj: (hh, 0, kj[n]))
    grid_spec = pltpu.PrefetchScalarGridSpec(
        num_scalar_prefetch=2, grid=(h, len(pairs)),
        in_specs=[qspec, kspec, kspec, qcol, krow, qspec, qcol, qspec],
        out_specs=[pl.BlockSpec((None, s, dh), lambda hh, n, qi, kj: (hh, 0, 0)), kspec, kspec,
                   krow],
        scratch_shapes=[pltpu.VMEM((t, dh), F32), pltpu.VMEM((t, dh), F32),
                        pltpu.VMEM((1, t), F32)])
    return pl.pallas_call(
        body, name=name, grid_spec=grid_spec,
        out_shape=[jax.ShapeDtypeStruct((h, s, dh), F32), jax.ShapeDtypeStruct((h, s, dh), F32),
                   jax.ShapeDtypeStruct((h, s, dh), F32), jax.ShapeDtypeStruct((h, 1, s), F32)],
        compiler_params=_params("parallel", "arbitrary"))(
            qi_tab, kj_tab, q, k, v, c_col, c_row, o, lse, do)


def _gate_fwd(zr, yp, yf, name):
    s, d = yp.shape
    tm = _pick(s, (256, 128))
    row = pl.BlockSpec((tm, d), lambda i: (i, 0))

    def body(zp_ref, zf_ref, yp_ref, yf_ref, o_ref):
        o_ref[...] = (jax.nn.sigmoid(zp_ref[...]) * yp_ref[...]
                      + jax.nn.sigmoid(zf_ref[...]) * yf_ref[...]).astype(BF16)

    return pl.pallas_call(
        body, name=name, grid=(s // tm,),
        in_specs=[pl.BlockSpec((tm, d), lambda i: (i, ZR_GP // d)),
                  pl.BlockSpec((tm, d), lambda i: (i, ZR_GF // d)), row, row],
        out_specs=row, out_shape=jax.ShapeDtypeStruct((s, d), BF16),
        compiler_params=_params("parallel"))(zr, zr, yp, yf)


def _gate_bwd(dm, zr, yp, yf, name):
    s, d = yp.shape
    tm = _pick(s, (256, 128))
    row = pl.BlockSpec((tm, d), lambda i: (i, 0))

    def body(dm_ref, zp_ref, zf_ref, yp_ref, yf_ref, dyp_ref, dyf_ref, dz_ref):
        dmv = dm_ref[...]
        gp, gf = jax.nn.sigmoid(zp_ref[...]), jax.nn.sigmoid(zf_ref[...])
        dyp_ref[...] = (dmv * gp).astype(BF16)
        dyf_ref[...] = (dmv * gf).astype(BF16)
        dz_ref[:, :d] = (dmv * yp_ref[...] * gp * (1.0 - gp)).astype(BF16)
        dz_ref[:, d:] = (dmv * yf_ref[...] * gf * (1.0 - gf)).astype(BF16)

    return pl.pallas_call(
        body, name=name, grid=(s // tm,),
        in_specs=[row, pl.BlockSpec((tm, d), lambda i: (i, ZR_GP // d)),
                  pl.BlockSpec((tm, d), lambda i: (i, ZR_GF // d)), row, row],
        out_specs=[row, row, pl.BlockSpec((tm, 2 * d), lambda i: (i, 0))],
        out_shape=[jax.ShapeDtypeStruct((s, d), BF16), jax.ShapeDtypeStruct((s, d), BF16),
                   jax.ShapeDtypeStruct((s, 2 * d), BF16)],
        compiler_params=_params("parallel"))(dm, zr, zr, yp, yf)


def _xattn_probs(q, k):
    sc = lax.dot_general(q, k, _DIMS["nt"], preferred_element_type=F32) * (1.0 / math.sqrt(X_DH))
    p = jnp.exp(sc - jnp.max(sc, axis=-1, keepdims=True))
    return p / jnp.sum(p, axis=-1, keepdims=True)


def _xattn_fwd(q, kv, name):
    s = q.shape[0]
    m = kv.shape[0]
    tq = _pick(s, (512, 256, 128))

    def body(q_ref, k_ref, v_ref, o_ref):
        p = _xattn_probs(q_ref[...], k_ref[...])
        o_ref[...] = jnp.dot(p.astype(BF16), v_ref[...],
                             preferred_element_type=F32).astype(BF16)

    qspec = pl.BlockSpec((tq, X_DH), lambda i, hh: (i, hh))
    return pl.pallas_call(
        body, name=name, grid=(s // tq, X_HEADS),
        in_specs=[qspec, pl.BlockSpec((m, X_DH), lambda i, hh: (0, hh)),
                  pl.BlockSpec((m, X_DH), lambda i, hh: (0, X_HEADS + hh))],
        out_specs=qspec, out_shape=jax.ShapeDtypeStruct((s, X_W), BF16),
        compiler_params=_params("parallel", "parallel"))(q, kv, kv)


def _xattn_bwd(q, kv, do, name):
    s = q.shape[0]
    m = kv.shape[0]
    tq = _pick(s, (512, 256, 128))
    scale = 1.0 / math.sqrt(X_DH)

    def body(q_ref, k_ref, v_ref, do_ref, dq_ref, dk_ref, dv_ref):
        @pl.when(pl.program_id(1) == 0)
        def _():
            dk_ref[...] = jnp.zeros_like(dk_ref)
            dv_ref[...] = jnp.zeros_like(dv_ref)

        qv, kk, dov = q_ref[...], k_ref[...], do_ref[...]
        p = _xattn_probs(qv, kk)
        dv_ref[...] += lax.dot_general(p.astype(BF16), dov, _DIMS["tn"],
                                       preferred_element_type=F32)
        dp = lax.dot_general(dov, v_ref[...], _DIMS["nt"], preferred_element_type=F32)
        ds = p * (dp - jnp.sum(dp * p, axis=-1, keepdims=True))
        dsb = (ds * scale).astype(BF16)
        dq_ref[...] = jnp.dot(dsb, kk, preferred_element_type=F32).astype(BF16)
        dk_ref[...] += lax.dot_general(dsb, qv, _DIMS["tn"], preferred_element_type=F32)

    qspec = pl.BlockSpec((tq, X_DH), lambda hh, i: (i, hh))
    kspec = pl.BlockSpec((m, X_DH), lambda hh, i: (0, hh))
    return pl.pallas_call(
        body, name=name, grid=(X_HEADS, s // tq),
        in_specs=[qspec, kspec, pl.BlockSpec((m, X_DH), lambda hh, i: (0, X_HEADS + hh)), qspec],
        out_specs=[qspec, kspec, kspec],
        out_shape=[jax.ShapeDtypeStruct((s, X_W), BF16), jax.ShapeDtypeStruct((m, X_W), F32),
                   jax.ShapeDtypeStruct((m, X_W), F32)],
        compiler_params=_params("parallel", "arbitrary"))(q, kv, kv, do)


_GELU_C = math.sqrt(2.0 / math.pi)
CONV_HALO = 16


def _gelu(x):
    return 0.5 * x * (1.0 + jnp.tanh(_GELU_C * (x + 0.044715 * x * x * x)))


def _gelu_grad(x):
    th = jnp.tanh(_GELU_C * (x + 0.044715 * x * x * x))
    return 0.5 * (1.0 + th) + 0.5 * x * (1.0 - th * th) * _GELU_C * (1.0 + 3 * 0.044715 * x * x)


def _shift_down(z, halo, n):
    rows = lax.broadcasted_iota(jnp.int32, z.shape, 0)
    out = pltpu.roll(z, n, 0)
    for r in range(n):
        out = jnp.where(rows == r, halo[CONV_HALO - n + r:CONV_HALO - n + r + 1], out)
    return out


def _conv(z, halo, cw, cb):
    return cw[2:3] * z + cw[1:2] * _shift_down(z, halo, 1) + cw[0:1] * _shift_down(z, halo, 2) + cb


def _conv_specs(tm, tn, off):
    per = tm // CONV_HALO
    return [pl.BlockSpec((tm, tn), lambda j, i: (i, j + off)),
            pl.BlockSpec((CONV_HALO, tn), lambda j, i: (jnp.maximum(i * per - 1, 0), j + off)),
            pl.BlockSpec((3, tn), lambda j, i: (0, j + off)),
            pl.BlockSpec((1, tn), lambda j, i: (0, j + off))]


def _convglu_fwd(z, cw, cb, name):
    s = z.shape[0]
    tm = _pick(s, (512, 256, 128))
    tn = 256
    nj = D_FF // tn

    def body(zg_ref, hg_ref, cwg_ref, cbg_ref, zu_ref, hu_ref, cwu_ref, cbu_ref, a_ref):
        first = pl.program_id(1) == 0
        gc = _conv(zg_ref[...].astype(F32), jnp.where(first, 0.0, hg_ref[...].astype(F32)),
                   cwg_ref[...], cbg_ref[...])
        uc = _conv(zu_ref[...].astype(F32), jnp.where(first, 0.0, hu_ref[...].astype(F32)),
                   cwu_ref[...], cbu_ref[...])
        a_ref[...] = (_gelu(gc) * uc).astype(BF16)

    return pl.pallas_call(
        body, name=name, grid=(nj, s // tm),
        in_specs=_conv_specs(tm, tn, 0) + _conv_specs(tm, tn, nj),
        out_specs=pl.BlockSpec((tm, tn), lambda j, i: (i, j)),
        out_shape=jax.ShapeDtypeStruct((s, D_FF), BF16),
        compiler_params=_params("parallel", "parallel"))(z, z, cw, cb, z, z, cw, cb)


def _convglu_bwd_pre(z, da, cw, cb, name):
    s = z.shape[0]
    tm = _pick(s, (512, 256, 128))
    tn = 256
    nj = D_FF // tn

    def body(zg_ref, hg_ref, cwg_ref, cbg_ref, zu_ref, hu_ref, cwu_ref, cbu_ref, da_ref,
             dg_ref, du_ref, dcwg_ref, dcwu_ref, dcbg_ref, dcbu_ref):
        first = pl.program_id(1) == 0

        @pl.when(first)
        def _():
            for r in (dcwg_ref, dcwu_ref, dcbg_ref, dcbu_ref):
                r[...] = jnp.zeros_like(r)

        dav = da_ref[...].astype(F32)
        zg, zu = zg_ref[...].astype(F32), zu_ref[...].astype(F32)
        hg = jnp.where(first, 0.0, hg_ref[...].astype(F32))
        hu = jnp.where(first, 0.0, hu_ref[...].astype(F32))
        gc = _conv(zg, hg, cwg_ref[...], cbg_ref[...])
        uc = _conv(zu, hu, cwu_ref[...], cbu_ref[...])
        dgc = dav * uc * _gelu_grad(gc)
        duc = dav * _gelu(gc)
        dg_ref[...] = dgc.astype(BF16)
        du_ref[...] = duc.astype(BF16)
        for d, zz, hh, dcw_ref, dcb_ref in ((dgc, zg, hg, dcwg_ref, dcbg_ref),
                                            (duc, zu, hu, dcwu_ref, dcbu_ref)):
            dcw_ref[...] += jnp.concatenate(
                [jnp.sum(d * _shift_down(zz, hh, 2), axis=0, keepdims=True),
                 jnp.sum(d * _shift_down(zz, hh, 1), axis=0, keepdims=True),
                 jnp.sum(d * zz, axis=0, keepdims=True)], axis=0)
            dcb_ref[...] += jnp.sum(d, axis=0, keepdims=True)

    tile = pl.BlockSpec((tm, tn), lambda j, i: (i, j))
    wspec = pl.BlockSpec((3, tn), lambda j, i: (0, j))
    bspec = pl.BlockSpec((1, tn), lambda j, i: (0, j))
    return pl.pallas_call(
        body, name=name, grid=(nj, s // tm),
        in_specs=_conv_specs(tm, tn, 0) + _conv_specs(tm, tn, nj) + [tile],
        out_specs=[tile, tile, wspec, wspec, bspec, bspec],
        out_shape=[jax.ShapeDtypeStruct((s, D_FF), BF16), jax.ShapeDtypeStruct((s, D_FF), BF16),
                   jax.ShapeDtypeStruct((3, D_FF), F32), jax.ShapeDtypeStruct((3, D_FF), F32),
                   jax.ShapeDtypeStruct((1, D_FF), F32), jax.ShapeDtypeStruct((1, D_FF), F32)],
        compiler_params=_params("parallel", "arbitrary"))(z, z, cw, cb, z, z, cw, cb, da)


def _convglu_bwd_post(dzc, cw, off, name):
    s, n = dzc.shape
    tm = _pick(s, (512, 256, 128))
    tn = 256
    nt = s // tm
    per = tm // CONV_HALO

    def body(d_ref, n_ref, cw_ref, o_ref):
        d = d_ref[...].astype(F32)
        nxt = jnp.where(pl.program_id(1) == nt - 1, 0.0, n_ref[...].astype(F32))
        rows = lax.broadcasted_iota(jnp.int32, d.shape, 0)
        acc = cw_ref[2:3] * d
        for k in (1, 2):
            up = pltpu.roll(d, tm - k, 0)
            for r in range(k):
                up = jnp.where(rows == tm - k + r, nxt[r:r + 1], up)
            acc = acc + cw_ref[2 - k:3 - k] * up
        o_ref[...] = acc.astype(BF16)

    return pl.pallas_call(
        body, name=name, grid=(n // tn, nt),
        in_specs=[pl.BlockSpec((tm, tn), lambda j, i: (i, j)),
                  pl.BlockSpec((CONV_HALO, tn),
                               lambda j, i: (jnp.minimum((i + 1) * per, s // CONV_HALO - 1), j)),
                  pl.BlockSpec((3, tn), lambda j, i: (0, j + off))],
        out_specs=pl.BlockSpec((tm, tn), lambda j, i: (i, j)),
        out_shape=jax.ShapeDtypeStruct((s, n), BF16),
        compiler_params=_params("parallel", "parallel"))(dzc, dzc, cw)


ANY = pl.BlockSpec(memory_space=pl.ANY)


def _place():
    x, y, c = (lax.axis_index(a) for a in MESH_AXES)
    return x, y, c, [(1 - x, y), (x, 1 - y), (1 - x, 1 - y)]


def _allgather(src, name):
    r, cdim = src.shape

    def body(x_ref, out_ref, send_sems, recv_sems, local_sem):
        x, y, c, chips = _place()
        me, sibling = (x, y, c), (x, y, 1 - c)

        def row(px, py, pc):
            return out_ref.at[4 * px + 2 * py + pc]

        def copy(k, block, to, src_ref=None):
            return pltpu.make_async_remote_copy(
                src_ref=row(*block) if src_ref is None else src_ref, dst_ref=row(*block),
                send_sem=send_sems.at[k], recv_sem=recv_sems.at[k], device_id=to,
                device_id_type=MESH_ID)

        mine = pltpu.make_async_copy(x_ref, row(*me), local_sem)
        mine.start()
        first = [copy(0, me, sibling, src_ref=x_ref)]
        first += [copy(1 + j, me, (*chip, c), src_ref=x_ref) for j, chip in enumerate(chips)]
        for cp in first:
            cp.start()
        passed = [copy(4 + j, (*chip, c), sibling) for j, chip in enumerate(chips)]
        for j, chip in enumerate(chips):
            copy(1 + j, (*chip, c), me).wait_recv()
            passed[j].start()
        copy(0, sibling, me).wait_recv()
        for j, chip in enumerate(chips):
            copy(4 + j, (*chip, 1 - c), me).wait_recv()
        for cp in first + passed:
            cp.wait_send()
        mine.wait()

    return pl.pallas_call(
        body, name=name, in_specs=[ANY], out_specs=ANY,
        out_shape=jax.ShapeDtypeStruct((N_DEV, r, cdim), src.dtype),
        scratch_shapes=[pltpu.SemaphoreType.DMA((7,)), pltpu.SemaphoreType.DMA((7,)),
                        pltpu.SemaphoreType.DMA(())],
    )(src)


def _swap_with_sibling(buf, name):
    _, nchip, r, cdim = buf.shape

    def body(g_ref, rcv_ref, send_sems, recv_sems):
        x, y, c, _ = _place()
        copies = [pltpu.make_async_remote_copy(
            src_ref=g_ref.at[1 - c, k], dst_ref=rcv_ref.at[k], send_sem=send_sems.at[k],
            recv_sem=recv_sems.at[k], device_id=(x, y, 1 - c), device_id_type=MESH_ID)
            for k in range(nchip)]
        for cp in copies:
            cp.start()
        for cp in copies:
            cp.wait()

    return pl.pallas_call(
        body, name=name, in_specs=[ANY], out_specs=ANY,
        out_shape=jax.ShapeDtypeStruct((nchip, r, cdim), buf.dtype),
        scratch_shapes=[pltpu.SemaphoreType.DMA((nchip,)), pltpu.SemaphoreType.DMA((nchip,))],
    )(buf)


def _exchange_chips(buf, name):
    nchip, r, cdim = buf.shape

    def body(b_ref, rcv_ref, send_sems, recv_sems, local_sem):
        x, y, c, chips = _place()
        my_chip = 2 * x + y
        mine = pltpu.make_async_copy(b_ref.at[my_chip], rcv_ref.at[my_chip], local_sem)
        mine.start()
        copies = [pltpu.make_async_remote_copy(
            src_ref=b_ref.at[2 * px + py], dst_ref=rcv_ref.at[my_chip], send_sem=send_sems.at[j],
            recv_sem=recv_sems.at[j], device_id=(px, py, c), device_id_type=MESH_ID)
            for j, (px, py) in enumerate(chips)]
        for cp in copies:
            cp.start()
        for j, (px, py) in enumerate(chips):
            pltpu.make_async_remote_copy(
                src_ref=b_ref.at[my_chip], dst_ref=rcv_ref.at[2 * px + py],
                send_sem=send_sems.at[j], recv_sem=recv_sems.at[j], device_id=(px, py, c),
                device_id_type=MESH_ID).wait_recv()
        for cp in copies:
            cp.wait_send()
        mine.wait()

    return pl.pallas_call(
        body, name=name, in_specs=[ANY], out_specs=ANY,
        out_shape=jax.ShapeDtypeStruct((nchip, r, cdim), buf.dtype),
        scratch_shapes=[pltpu.SemaphoreType.DMA((3,)), pltpu.SemaphoreType.DMA((3,)),
                        pltpu.SemaphoreType.DMA(())],
    )(buf)


def _pair_add(buf, rcv, core, name):
    _, nchip, r, cdim = buf.shape
    tr = _pick(r, (FLAT_ROWS, 8))

    def body(c_ref, a_ref, b_ref, o_ref):
        o_ref[...] = (a_ref[...] + b_ref[...]).astype(BF16)

    grid_spec = pltpu.PrefetchScalarGridSpec(
        num_scalar_prefetch=1, grid=(nchip, r // tr),
        in_specs=[pl.BlockSpec((None, None, tr, cdim), lambda k, i, c_ref: (c_ref[0], k, i, 0)),
                  pl.BlockSpec((None, tr, cdim), lambda k, i, c_ref: (k, i, 0))],
        out_specs=pl.BlockSpec((None, tr, cdim), lambda k, i, c_ref: (k, i, 0)))
    return pl.pallas_call(
        body, name=name, grid_spec=grid_spec,
        out_shape=jax.ShapeDtypeStruct((nchip, r, cdim), BF16),
        compiler_params=_params("parallel", "parallel"))(core, buf, rcv)


def _adamw(parts, w, m, v, name):
    npart, r, cdim = parts.shape
    tr = _pick(r, (FLAT_ROWS, 64, 32, 8))
    c1 = 1.0 - ADAM_B1 ** ADAM_STEP
    c2 = 1.0 - ADAM_B2 ** ADAM_STEP

    def body(p_ref, w_ref, m_ref, v_ref, g_ref, d_ref, mo_ref, vo_ref):
        g = p_ref[0].astype(F32)
        for k in range(1, npart):
            g = g + p_ref[k].astype(F32)
        mn = ADAM_B1 * m_ref[...] + (1.0 - ADAM_B1) * g
        vn = ADAM_B2 * v_ref[...] + (1.0 - ADAM_B2) * (g * g)
        g_ref[...] = g
        mo_ref[...] = mn
        vo_ref[...] = vn
        d_ref[...] = -ADAM_LR * ((mn / c1) / (jnp.sqrt(vn / c2) + ADAM_EPS) + ADAM_WD * w_ref[...])

    row = pl.BlockSpec((tr, cdim), lambda i: (i, 0))
    return pl.pallas_call(
        body, name=name, grid=(r // tr,),
        in_specs=[pl.BlockSpec((npart, tr, cdim), lambda i: (0, i, 0)), row, row, row],
        out_specs=[row] * 4, out_shape=[jax.ShapeDtypeStruct((r, cdim), F32)] * 4,
        compiler_params=_params("parallel"))(parts, w, m, v)


def _piece_rows(shape):
    return shape[0] * _round_up(int(math.prod(shape[1:])), LANES) // LANES


def _to_rows(a, nlead):
    lead, depth = a.shape[:nlead], a.shape[nlead]
    per = int(math.prod(a.shape[nlead + 1:]))
    if per % LANES:
        a = jnp.pad(a.reshape(lead + (depth, per)),
                    [(0, 0)] * (nlead + 1) + [(0, _round_up(per, LANES) - per)])
    return a.reshape(lead + (-1, LANES))


def _pack(arrays, row_mult, nlead=0):
    rows = jnp.concatenate([_to_rows(a, nlead) for a in arrays], axis=nlead)
    pad = _round_up(rows.shape[nlead], row_mult) - rows.shape[nlead]
    return jnp.pad(rows, [(0, 0)] * nlead + [(0, pad), (0, 0)])


def _unpack(rows, shapes, nlead=0):
    lead = rows.shape[:nlead]
    out, r = [], 0
    for shp in shapes:
        n = _piece_rows(shp)
        piece = lax.slice_in_dim(rows, r, r + n, axis=nlead)
        per = int(math.prod(shp[1:]))
        if per % LANES:
            piece = piece.reshape(lead + (shp[0], -1))[..., :per]
        out.append(piece.reshape(lead + tuple(shp)))
        r += n
    return out


def _full_weight(piece, name, layer):
    blk = piece[:, layer]
    return jnp.concatenate([blk[dev] for dev in range(N_DEV)],
                           axis=1 if name in COL_SHARDED else 0)


def _by_destination(name, grads):
    g = jnp.stack(grads)
    if name in COL_SHARDED:
        cs = g.shape[2] // N_DEV
        return jnp.stack([g[:, :, dev * cs:(dev + 1) * cs] for dev in range(N_DEV)])
    rs = g.shape[1] // N_DEV
    return jnp.stack([g[:, dev * rs:(dev + 1) * rs] for dev in range(N_DEV)])


def _heads(a):
    s = a.shape[0]
    return a.reshape(s, FOX_HEADS, FOX_DH).transpose(1, 0, 2)


def _unheads(a):
    return a.transpose(1, 0, 2).reshape(a.shape[1], FOX_W)


def kernel(*args):
    p = dict(zip(INPUTS, args))
    x0 = p["x"][0]
    mem = p["mem"][0]
    tgt = p["loss_target"][0]
    s, d = x0.shape
    core = lax.axis_index("c").astype(jnp.int32).reshape(1)

    mm_names = tuple(n for n in SHARDED if n != "conv_w")
    shard_shapes = {n: p[n].shape for n in SHARDED}
    gathered = _allgather(_pack([p[n].astype(BF16) for n in mm_names], 8), "gather_weights")
    pieces = dict(zip(mm_names, _unpack(gathered, [shard_shapes[n] for n in mm_names], nlead=1)))
    conv_g = _allgather(_pack([p["conv_w"]], 8), "gather_conv_w")
    conv_piece = _unpack(conv_g, [shard_shapes["conv_w"]], nlead=1)[0]

    def weights_of(layer):
        w = {n: _full_weight(pieces[n], n, layer) for n in mm_names}
        w_in = w.pop("w_in")
        fpad = jnp.pad(w_in[:, OFF_F:OFF_GP], ((0, 0), (0, ZR_W - ZR_F - FOX_HEADS)))
        w["w_cat"] = jnp.concatenate(
            [w_in[:, OFF_Q:OFF_F], w_in[:, OFF_GP:OFF_GF], w_in[:, OFF_GF:IN_W],
             w_in[:, 0:OFF_Q], fpad], axis=1)
        w["conv_w"] = _full_weight(conv_piece, "conv_w", layer)
        w["conv_b"] = p["conv_b"][layer].reshape(1, -1)
        w["pool_w"] = p["pool_w"][layer].astype(BF16)
        w["pool_scale"] = p["pool_scale"][layer].reshape(1, -1)
        w["b128"] = jnp.pad(p["b_forget"][layer], (0, 128 - FOX_HEADS)).reshape(1, 128)
        return w

    saved = []
    _, h1 = _norm_fwd(x0, None, None, p["mix_pre_g"][0], "norm_first")
    x_in = x0
    y_final = None
    for l in range(DEPTH):
        w = weights_of(l)
        sv = {"w": w, "x0": x_in, "h1": h1}
        zqkv = _mm(h1, w["w_cat"][:, :QKV_W], "nn", BF16, "mm_qkv")
        zr = _mm(h1, w["w_cat"][:, QKV_W:], "nn", F32, "mm_zr")
        c = _forget_fwd(zr, w["b128"], "forget_fwd")
        pm = _pool_fwd(zr, w["pool_w"], w["pool_scale"], "pool_fwd")
        qh, kh, vh = (_heads(zqkv[:, k * FOX_W:(k + 1) * FOX_W]) for k in range(3))
        c_t = c[:, :FOX_HEADS].T
        c_col, c_row = c_t[:, :, None], c_t[:, None, :]
        oh, lse = _fox_fwd(qh, kh, vh, c_col, c_row, "fox_fwd")
        o = _unheads(oh).astype(BF16)
        yp = _mm(pm, w["w_pool_br"], "nn", F32, "mm_pool_br")
        yf = _mm(o, w["w_fox_br"], "nn", F32, "mm_fox_br")
        merged = _gate_fwd(zr, yp, yf, "gate_fwd")
        f1 = _mm(merged, w["w_mix_out"], "nn", F32, "mm_mix_out")
        x1, h2 = _norm_fwd(x_in, f1, p["mix_post_g"][l], p["xa_pre_g"][l], "norm_mix_xa")
        _, mem_n = _norm_fwd(mem, None, None, p["mem_g"][l], "norm_mem")
        q2 = _mm(h2, w["w_xq"], "nn", BF16, "mm_xq")
        kv = _mm(mem_n, w["w_xkv"], "nn", BF16, "mm_xkv")
        o2 = _xattn_fwd(q2, kv, "xattn_fwd")
        f2 = _mm(o2, w["w_xo"], "nn", F32, "mm_xo")
        x2, h3 = _norm_fwd(x1, f2, p["xa_post_g"][l], p["ffn_pre_g"][l], "norm_xa_ffn")
        z3 = _mm(h3, w["w_up"], "nn", BF16, "mm_up")
        a = _convglu_fwd(z3, w["conv_w"], w["conv_b"], "convglu_fwd")
        f3 = _mm(a, w["w_down"], "nn", F32, "mm_down")
        if l + 1 < DEPTH:
            x3, h1 = _norm_fwd(x2, f3, p["ffn_post_g"][l], p["mix_pre_g"][l + 1], "norm_ffn_mix")
        else:
            x3, _ = _norm_fwd(x2, f3, p["ffn_post_g"][l], None, "norm_last")
            y_final = x3
        sv.update(zr=zr, qh=qh, kh=kh, vh=vh, c_col=c_col, c_row=c_row, oh=oh, lse=lse, o=o,
                  pm=pm, yp=yp, yf=yf, merged=merged, f1=f1, x1=x1, h2=h2, mem_n=mem_n, q2=q2,
                  kv=kv, o2=o2, f2=f2, x2=x2, h3=h3, z3=z3, a=a, f3=f3)
        saved.append(sv)
        x_in = x3

    top = _norm_bwd("loss_head", y=y_final, tgt=tgt, f_prev=saved[-1]["f3"],
                    g_post=p["ffn_post_g"][DEPTH - 1])
    loss = lax.psum(top["loss"][0, 0], MESH_AXES)
    dres, df3 = top["dx"], top["df"]
    gw = {n: [None] * DEPTH for n in WEIGHTS}
    gw["ffn_post_g"][DEPTH - 1] = top["dg_post"]
    for l in reversed(range(DEPTH)):
        sv = saved[l]
        w = sv["w"]
        gw["w_down"][l] = _mm(sv["a"], df3, "tn", F32, "mm_d_w_down")
        da = _mm(df3, w["w_down"], "nt", BF16, "mm_d_a")
        dgc, duc, dcwg, dcwu, dcbg, dcbu = _convglu_bwd_pre(sv["z3"], da, w["conv_w"],
                                                            w["conv_b"], "convglu_bwd_pre")
        gw["conv_w"][l] = jnp.concatenate([dcwg, dcwu], axis=1)
        gw["conv_b"][l] = jnp.concatenate([dcbg, dcbu], axis=1)
        dz3 = jnp.concatenate(
            [_convglu_bwd_post(dgc, w["conv_w"], 0, "convglu_bwd_post_g"),
             _convglu_bwd_post(duc, w["conv_w"], D_FF // 256, "convglu_bwd_post_u")], axis=1)
        gw["w_up"][l] = _mm(sv["h3"], dz3, "tn", F32, "mm_d_w_up")
        dh3 = _mm(dz3, w["w_up"], "nt", F32, "mm_d_h3")
        nb = _norm_bwd("norm_bwd_ffn_xa", dh=dh3, x=sv["x2"], g_pre=p["ffn_pre_g"][l], dres=dres,
                       f_prev=sv["f2"], g_post=p["xa_post_g"][l])
        gw["ffn_pre_g"][l], gw["xa_post_g"][l] = nb["dg_pre"], nb["dg_post"]
        dres, df2 = nb["dx"], nb["df"]
        gw["w_xo"][l] = _mm(sv["o2"], df2, "tn", F32, "mm_d_w_xo")
        do2 = _mm(df2, w["w_xo"], "nt", BF16, "mm_d_o2")
        dq2, dk2, dv2 = _xattn_bwd(sv["q2"], sv["kv"], do2, "xattn_bwd")
        dkv = jnp.concatenate([dk2, dv2], axis=1)
        gw["w_xq"][l] = _mm(sv["h2"], dq2, "tn", F32, "mm_d_w_xq")
        dh2 = _mm(dq2, w["w_xq"], "nt", F32, "mm_d_h2")
        gw["w_xkv"][l] = _mm(sv["mem_n"], dkv, "tn", F32, "mm_d_w_xkv")
        dmem_n = _mm(dkv, w["w_xkv"], "nt", F32, "mm_d_mem")
        gw["mem_g"][l] = _rms_dg(mem, p["mem_g"][l], dmem_n, "norm_mem_bwd")
        nb = _norm_bwd("norm_bwd_xa_mix", dh=dh2, x=sv["x1"], g_pre=p["xa_pre_g"][l], dres=dres,
                       f_prev=sv["f1"], g_post=p["mix_post_g"][l])
        gw["xa_pre_g"][l], gw["mix_post_g"][l] = nb["dg_pre"], nb["dg_post"]
        dres, df1 = nb["dx"], nb["df"]
        gw["w_mix_out"][l] = _mm(sv["merged"], df1, "tn", F32, "mm_d_w_mix_out")
        dmerged = _mm(df1, w["w_mix_out"], "nt", F32, "mm_d_merged")
        dyp, dyf, dzg = _gate_bwd(dmerged, sv["zr"], sv["yp"], sv["yf"], "gate_bwd")
        gw["w_pool_br"][l] = _mm(sv["pm"], dyp, "tn", F32, "mm_d_w_pool_br")
        dpm = _mm(dyp, w["w_pool_br"], "nt", F32, "mm_d_pm")
        gw["w_fox_br"][l] = _mm(sv["o"], dyf, "tn", F32, "mm_d_w_fox_br")
        do = _mm(dyf, w["w_fox_br"], "nt", BF16, "mm_d_o")
        du, dpw, dsc = _pool_bwd(sv["zr"], dpm, w["pool_w"], w["pool_scale"], "pool_bwd")
        gw["pool_w"][l], gw["pool_scale"][l] = dpw, dsc
        dqh, dkh, dvh, dc = _fox_bwd(sv["qh"], sv["kh"], sv["vh"], sv["c_col"], sv["c_row"],
                                     sv["oh"], sv["lse"], _heads(do), "fox_bwd")
        dc_rows = jnp.pad(dc[:, 0, :].T, ((0, 0), (0, 128 - FOX_HEADS)))
        dzf, db = _forget_bwd(dc_rows, sv["zr"], w["b128"], "forget_bwd")
        gw["b_forget"][l] = db[0, :FOX_HEADS]
        dzf_pad = jnp.pad(dzf[:, :FOX_HEADS], ((0, 0), (0, ZR_W - ZR_F - FOX_HEADS)))
        dzc = jnp.concatenate([_unheads(dqh).astype(BF16), _unheads(dkh).astype(BF16),
                               _unheads(dvh).astype(BF16), dzg, du, dzf_pad.astype(BF16)], axis=1)
        dw_cat = _mm(sv["h1"], dzc, "tn", F32, "mm_d_w_in")
        gw["w_in"][l] = jnp.concatenate(
            [dw_cat[:, QKV_W + ZR_U:QKV_W + ZR_F], dw_cat[:, :QKV_W],
             dw_cat[:, QKV_W + ZR_F:QKV_W + ZR_F + FOX_HEADS],
             dw_cat[:, QKV_W + ZR_GP:QKV_W + ZR_U]], axis=1)
        dh1 = _mm(dzc, w["w_cat"], "nt", F32, "mm_d_h1")
        if l > 0:
            nb = _norm_bwd("norm_bwd_mix_ffn", dh=dh1, x=sv["x0"], g_pre=p["mix_pre_g"][l],
                           dres=dres, f_prev=saved[l - 1]["f3"], g_post=p["ffn_post_g"][l - 1])
            gw["ffn_post_g"][l - 1] = nb["dg_post"]
            df3 = nb["df"]
        else:
            nb = _norm_bwd("norm_bwd_first", dh=dh1, x=sv["x0"], g_pre=p["mix_pre_g"][l], dres=dres)
        gw["mix_pre_g"][l] = nb["dg_pre"]
        dres = nb["dx"]
    grad_x = dres[None]

    by_dest = _pack([_by_destination(n, gw[n]) for n in SHARDED], FLAT_ROWS, nlead=1)
    buf = by_dest.reshape(4, 2, -1, LANES).transpose(1, 0, 2, 3)
    rcv = _swap_with_sibling(buf, "grads_to_sibling")
    pair = _pair_add(buf, rcv, core, "grads_pair_add")
    parts = _exchange_chips(pair, "grads_to_chips")
    shapes = [shard_shapes[n] for n in SHARDED]
    packed = [_pack([p[pre + n] for n in SHARDED], FLAT_ROWS) for pre in ("", "m_", "v_")]
    res_sh = [dict(zip(SHARDED, _unpack(r, shapes)))
              for r in _adamw(parts, *packed, "adamw_sharded")]

    rep_shapes = [p[n].shape for n in REPLICATED]
    part = _pack([jnp.stack(gw[n]).reshape(p[n].shape) for n in REPLICATED], 8)
    allparts = _allgather(part, "gather_small_grads")
    packed = [_pack([p[pre + n] for n in REPLICATED], 8) for pre in ("", "m_", "v_")]
    res_rep = [dict(zip(REPLICATED, _unpack(r, rep_shapes)))
               for r in _adamw(allparts, *packed, "adamw_replicated")]

    outs = [loss, grad_x]
    for k in range(4):
        outs += [res_sh[k][n] if n in SHARDED else res_rep[k][n] for n in WEIGHTS]
    return tuple(outs)
```
